```python
import math, functools
import jax, jax.numpy as jnp
from jax import lax
import numpy as np

D_MODEL = 2048
BATCH = 1
SEQ = 8192
DEPTH = 1
DEC_BATCH = 128
DEC_SEQ = 1
PAST_LEN = 16384
PAGE_SIZE = 128

MLA_HEADS = 16
QK_NOPE = 128
QK_ROPE = 64
V_DIM = 128
Q_LORA = 768
KV_LORA = 512
ROPE_THETA = 10000.0
ATTN_SCALE = (QK_NOPE + QK_ROPE) ** -0.5
Q_BLOCK = 128
SSM_HEADS = 32
SSM_HEAD_DIM = 64
D_INNER = SSM_HEADS * SSM_HEAD_DIM
SSM_GROUPS = 4
HEADS_PER_GROUP = SSM_HEADS // SSM_GROUPS
SSM_STATE = 128
SSM_CONV = 4
CONV_DIM = D_INNER + 2 * SSM_GROUPS * SSM_STATE
SSD_CHUNK = 128
D_FF = 4 * D_MODEL
FFN_CONV = 3
PLE_DIM = 256
EPS = 1e-6
IN_SPLITS = (Q_LORA, KV_LORA, QK_ROPE, D_INNER, CONV_DIM, SSM_HEADS, D_MODEL, D_MODEL)
IN_COLS = sum(IN_SPLITS)
IN_OFFSETS = tuple(int(v) for v in np.cumsum(IN_SPLITS)[:-1])

kernel_name = 'gated_mla_ssd_convffn_step'


def rms_norm(x, w):
    x32 = x.astype(jnp.float32)
    y = x32 * lax.rsqrt(jnp.mean(x32 * x32, axis=-1, keepdims=True) + EPS)
    return y.astype(x.dtype) * w


def group_rms_norm(x, w, groups):
    shp = x.shape
    x32 = x.reshape(shp[:-1] + (groups, shp[-1] // groups)).astype(jnp.float32)
    y = x32 * lax.rsqrt(jnp.mean(x32 * x32, axis=-1, keepdims=True) + EPS)
    return y.reshape(shp).astype(x.dtype) * w


def rope(x, pos):
    half = x.shape[-1] // 2
    inv = ROPE_THETA ** (-jnp.arange(half, dtype=jnp.float32) / half)
    ang = pos[:, None] * inv[None, :]
    ang = ang.reshape((ang.shape[0],) + (1,) * (x.ndim - 3) + (half,))
    cos = jnp.cos(ang).astype(x.dtype)
    sin = jnp.sin(ang).astype(x.dtype)
    x1, x2 = x[..., :half], x[..., half:]
    return jnp.concatenate([x1 * cos - x2 * sin, x2 * cos + x1 * sin], axis=-1)


def causal_dwconv(x, buf, w, b):
    width = w.shape[0]
    s = x.shape[1]
    xp = jnp.concatenate([buf.astype(x.dtype), x], axis=1)
    y = xp[:, :s] * w[0]
    for k in range(1, width):
        y = y + xp[:, k:k + s] * w[k]
    return y + b, xp[:, s:]


def latent_attention(q_lat, q_rope, c, kr, mask):
    s = (jnp.einsum('nqhc,nkc->nhqk', q_lat, c)
         + jnp.einsum('nqhr,nkr->nhqk', q_rope, kr)).astype(jnp.float32) * ATTN_SCALE
    p = jax.nn.softmax(jnp.where(mask, s, -jnp.inf), axis=-1).astype(c.dtype)
    return jnp.einsum('nhqk,nkc->nqhc', p, c)


def mla_prompt_attention(q_lat, q_rope, c, kr):
    n, s = q_lat.shape[:2]
    nb = s // Q_BLOCK
    qb = q_lat.reshape(n, nb, Q_BLOCK, MLA_HEADS, KV_LORA).transpose(1, 0, 2, 3, 4)
    rb = q_rope.reshape(n, nb, Q_BLOCK, MLA_HEADS, QK_ROPE).transpose(1, 0, 2, 3, 4)
    k_pos = jnp.arange(s)

    def block(args):
        i, ql, qr = args
        q_pos = i * Q_BLOCK + jnp.arange(Q_BLOCK)
        return latent_attention(ql, qr, c, kr, k_pos[None, :] <= q_pos[:, None])

    out = lax.map(block, (jnp.arange(nb), qb, rb))
    return out.transpose(1, 0, 2, 3, 4).reshape(n, s, MLA_HEADS, KV_LORA)


def mla_sample_attention(cache_c, cache_kr, page_table, layer, q_lat, q_rope, c_new, kr_new):
    s = q_lat.shape[1]
    past = page_table.shape[1] * PAGE_SIZE
    k_pos = jnp.arange(past + s)
    q_pos = past + jnp.arange(s)
    mask = k_pos[None, :] <= q_pos[:, None]

    def one_seq(args):
        pages, ql, qr, cn, krn = args
        c_past = cache_c[layer, pages].reshape(past, KV_LORA).astype(cn.dtype)
        kr_past = cache_kr[layer, pages].reshape(past, QK_ROPE).astype(krn.dtype)
        c = jnp.concatenate([c_past, cn], axis=0)
        kr = jnp.concatenate([kr_past, krn], axis=0)
        return latent_attention(ql[None], qr[None], c[None], kr[None], mask)[0]

    return lax.map(one_seq, (page_table, q_lat, q_rope, c_new, kr_new))


def ssd_chunked(x, dt, a, b, c):
    n, T = x.shape[:2]
    L = SSD_CHUNK
    nc = T // L
    xd = (x * dt[..., None]).reshape(n, nc, L, SSM_HEADS, SSM_HEAD_DIM)
    da = (dt * a).reshape(n, nc, L, SSM_HEADS).transpose(0, 3, 1, 2)
    bb = b.reshape(n, nc, L, SSM_HEADS, SSM_STATE)
    cc = c.reshape(n, nc, L, SSM_HEADS, SSM_STATE)
    cs = jnp.cumsum(da, axis=-1)
    tri = jnp.tril(jnp.ones((L, L), dtype=bool))
    decay = jnp.exp(jnp.where(tri, cs[..., :, None] - cs[..., None, :], -jnp.inf))
    scores = jnp.einsum('nclhk,ncshk->nhcls', cc, bb) * decay
    y_diag = jnp.einsum('nhcls,ncshp->nclhp', scores, xd)
    states = jnp.einsum('ncshk,nhcs,ncshp->nchpk', bb, jnp.exp(cs[..., -1:] - cs), xd)
    chunk_decay = jnp.exp(cs[..., -1])

    def carry(h, inp):
        st, dec = inp
        return h * dec[..., None, None] + st, h

    h0 = jnp.zeros((n, SSM_HEADS, SSM_HEAD_DIM, SSM_STATE), x.dtype)
    h_final, h_prev = lax.scan(carry, h0, (states.transpose(1, 0, 2, 3, 4), chunk_decay.transpose(2, 0, 1)))
    y_off = jnp.einsum('nclhk,cnhpk,nhcl->nclhp', cc, h_prev, jnp.exp(cs))
    return (y_diag + y_off).reshape(n, T, SSM_HEADS, SSM_HEAD_DIM), h_final


def ssd_recurrent(x, dt, a, b, c, h0):
    def step(h, inp):
        xt, dtt, bt, ct = inp
        h = h * jnp.exp(dtt * a)[..., None, None] + jnp.einsum('nhp,nhk->nhpk', xt * dtt[..., None], bt)
        return h, jnp.einsum('nhpk,nhk->nhp', h, ct)

    h, y = lax.scan(step, h0, (x.swapaxes(0, 1), dt.swapaxes(0, 1), b.swapaxes(0, 1), c.swapaxes(0, 1)))
    return y.swapaxes(0, 1), h


def decoder_layer(h, pe, pos, w, ssm_conv_buf, ffn_conv_buf, ssm_h0, attend):
    n, s = h.shape[:2]
    f32 = jnp.float32
    a = rms_norm(h, w['norm_pre_mix'])
    q_c, kv_c, kr_raw, z, xbc, dt_raw, g_attn, g_ssm = jnp.split(a @ w['w_in'], IN_OFFSETS, axis=-1)

    q = (rms_norm(q_c, w['norm_q']) @ w['w_q_up']).reshape(n, s, MLA_HEADS, QK_NOPE + QK_ROPE)
    q_nope = q[..., :QK_NOPE]
    q_rope = rope(q[..., QK_NOPE:], pos)
    c = rms_norm(kv_c, w['norm_kv'])
    kr = rope(kr_raw, pos)
    w_kv = w['w_kv_up'].reshape(KV_LORA, MLA_HEADS, QK_NOPE + V_DIM)
    q_lat = jnp.einsum('nshd,chd->nshc', q_nope, w_kv[..., :QK_NOPE])
    o_lat = attend(q_lat, q_rope, c, kr)
    o_attn = jnp.einsum('nshc,chv->nshv', o_lat, w_kv[..., QK_NOPE:]).reshape(n, s, MLA_HEADS * V_DIM)

    xbc, ssm_conv_new = causal_dwconv(xbc, ssm_conv_buf, w['conv_ssm_w'], w['conv_ssm_b'])
    xbc = jax.nn.silu(xbc)
    gn = SSM_GROUPS * SSM_STATE
    xs = xbc[..., :D_INNER].reshape(n, s, SSM_HEADS, SSM_HEAD_DIM).astype(f32)
    bh = jnp.repeat(xbc[..., D_INNER:D_INNER + gn].reshape(n, s, SSM_GROUPS, SSM_STATE), HEADS_PER_GROUP, axis=2).astype(f32)
    ch = jnp.repeat(xbc[..., D_INNER + gn:].reshape(n, s, SSM_GROUPS, SSM_STATE), HEADS_PER_GROUP, axis=2).astype(f32)
    dt = jax.nn.softplus(dt_raw.astype(f32) + w['dt_bias'].astype(f32))
    a_neg = -jnp.exp(w['a_log'].astype(f32))
    if ssm_h0 is None:
        y, h_ssm = ssd_chunked(xs, dt, a_neg, bh, ch)
    else:
        y, h_ssm = ssd_recurrent(xs, dt, a_neg, bh, ch, ssm_h0.astype(f32))
    y = (y + w['d_skip'].astype(f32)[:, None] * xs).reshape(n, s, D_INNER).astype(h.dtype)
    o_ssm = group_rms_norm(y * jax.nn.silu(z), w['norm_ssm'], SSM_GROUPS)

    merged = (jax.nn.sigmoid(g_attn) * (o_attn @ w['w_branch_attn'])
              + jax.nn.sigmoid(g_ssm) * (o_ssm @ w['w_branch_ssm']))
    h = h + rms_norm(merged @ w['w_out'], w['norm_post_mix'])

    u = rms_norm(h, w['norm_pre_ffn']) @ w['w_ffn_up']
    u, ffn_conv_new = causal_dwconv(u, ffn_conv_buf, w['conv_ffn_w'], w['conv_ffn_b'])
    f = (jax.nn.gelu(u[..., :D_FF], approximate=True) * u[..., D_FF:]) @ w['w_ffn_down']
    h = h + rms_norm(f, w['norm_post_ffn'])

    h = h + rms_norm(jax.nn.sigmoid(h @ w['w_ple_gate']) * (pe @ w['w_ple_proj']), w['norm_ple'])
    return h, (c, kr, h_ssm.astype(h.dtype), ssm_conv_new, ffn_conv_new)


def setup_inputs(seed: int = 0) -> dict:
    key = jax.random.key(seed)
    ks = iter(jax.random.split(key, 64))
    f32 = jnp.float32

    def nrm(shape, scale=1.0):
        return jax.random.normal(next(ks), shape, f32) * scale

    def gain(dim):
        return 1.0 + nrm((DEPTH, dim), 0.05)

    n_pages = PAST_LEN // PAGE_SIZE
    n_used = DEC_BATCH * n_pages
    n_pool = n_used + (n_used + 3) // 4
    page_table = jax.random.permutation(next(ks), n_pool)[:n_used].reshape(DEC_BATCH, n_pages).astype(jnp.int32)
    dt0 = jnp.exp(jax.random.uniform(next(ks), (DEPTH, SSM_HEADS), f32, math.log(1e-3), math.log(1e-1)))
    dt_bias = dt0 + jnp.log(-jnp.expm1(-dt0))
    a_log = jnp.log(jax.random.uniform(next(ks), (DEPTH, SSM_HEADS), f32, 1.0, 16.0))
    return {
        'x_prompt': nrm((BATCH, SEQ, D_MODEL)),
        'x_sample': nrm((DEC_BATCH, DEC_SEQ, D_MODEL)),
        'cache_kv_latent': nrm((DEPTH, n_pool, PAGE_SIZE, KV_LORA)),
        'cache_k_rope': nrm((DEPTH, n_pool, PAGE_SIZE, QK_ROPE)),
        'state_ssm': nrm((DEPTH, DEC_BATCH, SSM_HEADS, SSM_HEAD_DIM, SSM_STATE), 0.1),
        'state_ssm_conv': nrm((DEPTH, DEC_BATCH, SSM_CONV - 1, CONV_DIM)),
        'state_ffn_conv': nrm((DEPTH, DEC_BATCH, FFN_CONV - 1, 2 * D_FF)),
        'page_table': page_table,
        'p_prompt': nrm((DEPTH, BATCH, SEQ, PLE_DIM)),
        'p_sample': nrm((DEPTH, DEC_BATCH, DEC_SEQ, PLE_DIM)),
        'norm_pre_mix': gain(D_MODEL),
        'w_in': nrm((DEPTH, D_MODEL, IN_COLS), D_MODEL ** -0.5),
        'norm_q': gain(Q_LORA),
        'w_q_up': nrm((DEPTH, Q_LORA, MLA_HEADS * (QK_NOPE + QK_ROPE)), Q_LORA ** -0.5),
        'norm_kv': gain(KV_LORA),
        'w_kv_up': nrm((DEPTH, KV_LORA, MLA_HEADS * (QK_NOPE + V_DIM)), KV_LORA ** -0.5),
        'w_branch_attn': nrm((DEPTH, MLA_HEADS * V_DIM, D_MODEL), (MLA_HEADS * V_DIM) ** -0.5),
        'conv_ssm_w': nrm((DEPTH, SSM_CONV, CONV_DIM), SSM_CONV ** -0.5),
        'conv_ssm_b': nrm((DEPTH, CONV_DIM), 0.02),
        'dt_bias': dt_bias,
        'a_log': a_log,
        'd_skip': gain(SSM_HEADS),
        'norm_ssm': gain(D_INNER),
        'w_branch_ssm': nrm((DEPTH, D_INNER, D_MODEL), D_INNER ** -0.5),
        'w_out': nrm((DEPTH, D_MODEL, D_MODEL), D_MODEL ** -0.5),
        'norm_post_mix': gain(D_MODEL),
        'norm_pre_ffn': gain(D_MODEL),
        'w_ffn_up': nrm((DEPTH, D_MODEL, 2 * D_FF), D_MODEL ** -0.5),
        'conv_ffn_w': nrm((DEPTH, FFN_CONV, 2 * D_FF), FFN_CONV ** -0.5),
        'conv_ffn_b': nrm((DEPTH, 2 * D_FF), 0.02),
        'w_ffn_down': nrm((DEPTH, D_FF, D_MODEL), D_FF ** -0.5),
        'norm_post_ffn': gain(D_MODEL),
        'w_ple_gate': nrm((DEPTH, D_MODEL, D_MODEL), D_MODEL ** -0.5),
        'w_ple_proj': nrm((DEPTH, PLE_DIM, D_MODEL), PLE_DIM ** -0.5),
        'norm_ple': gain(D_MODEL),
    }


def reference(x_prompt, x_sample, cache_kv_latent, cache_k_rope, state_ssm, state_ssm_conv,
              state_ffn_conv, page_table, p_prompt, p_sample,
              norm_pre_mix, w_in, norm_q, w_q_up, norm_kv, w_kv_up, w_branch_attn,
              conv_ssm_w, conv_ssm_b, dt_bias, a_log, d_skip, norm_ssm, w_branch_ssm,
              w_out, norm_post_mix, norm_pre_ffn, w_ffn_up, conv_ffn_w, conv_ffn_b,
              w_ffn_down, norm_post_ffn, w_ple_gate, w_ple_proj, norm_ple):
    n_p, s_p = x_prompt.shape[:2]
    s_s = x_sample.shape[1]
    past = page_table.shape[1] * PAGE_SIZE
    pos_prompt = jnp.arange(s_p, dtype=jnp.float32)
    pos_sample = past + jnp.arange(s_s, dtype=jnp.float32)
    hp, hs = x_prompt, x_sample
    outs_p = ([], [], [], [], [])
    outs_s = ([], [], [], [], [])
    for d in range(DEPTH):
        w = dict(norm_pre_mix=norm_pre_mix[d], w_in=w_in[d], norm_q=norm_q[d], w_q_up=w_q_up[d],
                 norm_kv=norm_kv[d], w_kv_up=w_kv_up[d], w_branch_attn=w_branch_attn[d],
                 conv_ssm_w=conv_ssm_w[d], conv_ssm_b=conv_ssm_b[d], dt_bias=dt_bias[d],
                 a_log=a_log[d], d_skip=d_skip[d], norm_ssm=norm_ssm[d], w_branch_ssm=w_branch_ssm[d],
                 w_out=w_out[d], norm_post_mix=norm_post_mix[d], norm_pre_ffn=norm_pre_ffn[d],
                 w_ffn_up=w_ffn_up[d], conv_ffn_w=conv_ffn_w[d], conv_ffn_b=conv_ffn_b[d],
                 w_ffn_down=w_ffn_down[d], norm_post_ffn=norm_post_ffn[d],
                 w_ple_gate=w_ple_gate[d], w_ple_proj=w_ple_proj[d], norm_ple=norm_ple[d])
        hp, st_p = decoder_layer(hp, p_prompt[d], pos_prompt, w,
                                 jnp.zeros((n_p, SSM_CONV - 1, CONV_DIM), x_prompt.dtype),
                                 jnp.zeros((n_p, FFN_CONV - 1, 2 * D_FF), x_prompt.dtype),
                                 None, mla_prompt_attention)
        attend = functools.partial(mla_sample_attention, cache_kv_latent, cache_k_rope, page_table, d)
        hs, st_s = decoder_layer(hs, p_sample[d], pos_sample, w, state_ssm_conv[d], state_ffn_conv[d],
                                 state_ssm[d], attend)
        for lst, v in zip(outs_p, st_p):
            lst.append(v)
        for lst, v in zip(outs_s, st_s):
            lst.append(v)
    kv_p, kr_p, ssm_p, sconv_p, fconv_p = [jnp.stack(lst) for lst in outs_p]
    kv_s, kr_s, ssm_s, sconv_s, fconv_s = [jnp.stack(lst) for lst in outs_s]
    return (hp, hs, kv_p, kr_p, ssm_p, sconv_p, fconv_p, kv_s, kr_s, ssm_s, sconv_s, fconv_s)
```

```python
import functools

import jax
import jax.numpy as jnp
import numpy as np
from jax import lax
from jax.experimental import pallas as pl
from jax.experimental.pallas import tpu as pltpu

F32, BF16 = jnp.float32, jnp.bfloat16
HIGHEST = lax.Precision.HIGHEST

LANES = 128
SUBLANES = 8
BF16_SUBLANES = 16
V7X_VMEM_BYTES = 64 * 1024 * 1024

D_MODEL = 2048
PAGE = 128
HEADS = 16
QK_NOPE = 128
QK_ROPE = 64
V_DIM = 128
Q_LORA = 768
KV_LORA = 512
ROPE_THETA = 10000.0
ATTN_SCALE = (QK_NOPE + QK_ROPE) ** -0.5
SSM_HEADS = 32
SSM_P = 64
D_INNER = SSM_HEADS * SSM_P
SSM_GROUPS = 4
SSM_N = 128
SSM_CONV = 4
GN = SSM_GROUPS * SSM_N
CONV_DIM = D_INNER + 2 * GN
CHUNK = 128
PAIRS = SSM_HEADS // 2
PAIRS_PER_GROUP = PAIRS // SSM_GROUPS
D_FF = 4 * D_MODEL
FFN_CONV = 3
PLE_DIM = 256
EPS = 1e-6

P_Z = 0
P_GA = P_Z + D_INNER
P_GS = P_GA + D_MODEL
P_XBC = P_GS + D_MODEL
P_KV = P_XBC + CONV_DIM
P_SMALL = P_KV + KV_LORA
P_Q = P_SMALL + 2 * LANES
P_COLS = P_Q + Q_LORA
SMALL_W = 2 * LANES


def _params(n_axes, vmem_mib):
    return pltpu.CompilerParams(dimension_semantics=("arbitrary",) * n_axes,
                                vmem_limit_bytes=vmem_mib * 1024 * 1024)


def _rms(x, w):
    return x * lax.rsqrt(jnp.mean(x * x, axis=-1, keepdims=True) + EPS) * w


def _dot(a, b):
    return jnp.dot(a, b, preferred_element_type=F32)


def _dot_nt(a, b):
    return lax.dot_general(a, b, (((1,), (1,)), ((), ())), preferred_element_type=F32)


def _dot_exact(a, b):
    return jnp.dot(a, b, precision=HIGHEST, preferred_element_type=F32)


def _rep(x, n):
    return x if n == 1 else jnp.concatenate([x] * n, axis=1)


def _silu(x):
    return x * jax.nn.sigmoid(x)


def _softplus(x):
    return jnp.maximum(x, 0.0) + jnp.log1p(jnp.exp(-jnp.abs(x)))


def _gelu_tanh(x):
    return x * (0.5 * (1.0 + jnp.tanh(np.sqrt(2.0 / np.pi).astype(np.float32) * (x + 0.044715 * (x * x * x)))))


def _in_proj_kernel(x_ref, nw_ref, w_ref, o_ref, xn_ref):
    @pl.when(pl.program_id(1) == 0)
    def _():
        xn_ref[...] = _rms(x_ref[...], nw_ref[...]).astype(BF16)

    o_ref[...] = _dot(xn_ref[...], w_ref[...])


def _in_proj(x, nw, w_packed):
    t = x.shape[0]
    tm, tn = min(t, 1024), 512
    return pl.pallas_call(
        _in_proj_kernel,
        grid=(t // tm, P_COLS // tn),
        in_specs=[pl.BlockSpec((tm, D_MODEL), lambda i, j: (i, 0)),
                  pl.BlockSpec((1, D_MODEL), lambda i, j: (0, 0)),
                  pl.BlockSpec((D_MODEL, tn), lambda i, j: (0, j))],
        out_specs=pl.BlockSpec((tm, tn), lambda i, j: (i, j)),
        out_shape=jax.ShapeDtypeStruct((t, P_COLS), F32),
        scratch_shapes=[pltpu.VMEM((tm, D_MODEL), BF16)],
        compiler_params=_params(2, 48),
        name="in_proj",
    )(x, nw, w_packed)


def _kv_prep_kernel(kv_ref, sm_ref, cos_ref, sin_ref, nw_ref, c_ref, cbf_ref, kr_ref, krbf_ref):
    c = _rms(kv_ref[...], nw_ref[...])
    c_ref[...] = c
    cbf_ref[...] = c.astype(BF16)
    sm = sm_ref[...]
    kr = sm[:, 0:QK_ROPE] * cos_ref[...] + sm[:, QK_ROPE:2 * QK_ROPE] * sin_ref[...]
    kr_ref[...] = kr
    krbf_ref[...] = kr.astype(BF16)


def _kv_prep(proj, cos64, sin64, nw):
    t = proj.shape[0]
    tm = min(t, 512)
    return pl.pallas_call(
        _kv_prep_kernel,
        grid=(t // tm,),
        in_specs=[pl.BlockSpec((tm, KV_LORA), lambda i: (i, P_KV // KV_LORA)),
                  pl.BlockSpec((tm, SMALL_W), lambda i: (i, P_SMALL // SMALL_W)),
                  pl.BlockSpec((tm, QK_ROPE), lambda i: (i, 0)),
                  pl.BlockSpec((tm, QK_ROPE), lambda i: (i, 0)),
                  pl.BlockSpec((1, KV_LORA), lambda i: (0, 0))],
        out_specs=[pl.BlockSpec((tm, KV_LORA), lambda i: (i, 0)),
                   pl.BlockSpec((tm, KV_LORA), lambda i: (i, 0)),
                   pl.BlockSpec((tm, QK_ROPE), lambda i: (i, 0)),
                   pl.BlockSpec((tm, QK_ROPE), lambda i: (i, 0))],
        out_shape=[jax.ShapeDtypeStruct((t, KV_LORA), F32), jax.ShapeDtypeStruct((t, KV_LORA), BF16),
                   jax.ShapeDtypeStruct((t, QK_ROPE), F32), jax.ShapeDtypeStruct((t, QK_ROPE), BF16)],
        compiler_params=_params(1, 32),
        name="kv_prep",
    )(proj, proj, cos64, sin64, nw)


def _q_prep_kernel(q_ref, nw_ref, wn_ref, wr_ref, wrr_ref, cos_ref, sin_ref, qn_ref, qr_ref):
    qn = _rms(q_ref[...], nw_ref[...]).astype(BF16)
    qn_ref[...] = _dot(qn, wn_ref[...]).astype(BF16)
    r = _dot(qn, wr_ref[...])
    rr = _dot(qn, wrr_ref[...])
    cos, sin = cos_ref[...], sin_ref[...]
    for k in range(HEADS * QK_ROPE // LANES):
        sl = slice(k * LANES, (k + 1) * LANES)
        qr_ref[:, sl] = (r[:, sl] * cos + rr[:, sl] * sin).astype(BF16)


def _q_prep(proj, nw, wq_nope, wq_rope, wq_rope_rot, cos128, sin128):
    t = proj.shape[0]
    tm = min(t, 512)
    n_nope, n_rope = HEADS * QK_NOPE, HEADS * QK_ROPE
    return pl.pallas_call(
        _q_prep_kernel,
        grid=(t // tm,),
        in_specs=[pl.BlockSpec((tm, Q_LORA), lambda i: (i, P_Q // Q_LORA)),
                  pl.BlockSpec((1, Q_LORA), lambda i: (0, 0)),
                  pl.BlockSpec((Q_LORA, n_nope), lambda i: (0, 0)),
                  pl.BlockSpec((Q_LORA, n_rope), lambda i: (0, 0)),
                  pl.BlockSpec((Q_LORA, n_rope), lambda i: (0, 0)),
                  pl.BlockSpec((tm, LANES), lambda i: (i, 0)),
                  pl.BlockSpec((tm, LANES), lambda i: (i, 0))],
        out_specs=[pl.BlockSpec((tm, n_nope), lambda i: (i, 0)),
                   pl.BlockSpec((tm, n_rope), lambda i: (i, 0))],
        out_shape=[jax.ShapeDtypeStruct((t, n_nope), BF16), jax.ShapeDtypeStruct((t, n_rope), BF16)],
        compiler_params=_params(1, 48),
        name="q_prep",
    )(proj, nw, wq_nope, wq_rope, wq_rope_rot, cos128, sin128)


def _prompt_attn_kernel(qi_ref, kj_ref, qn_ref, qr_ref, wn_ref, wv_ref, c_ref, kr_ref, o_ref,
                        qlat_s, qrope_s, m_s, l_s, acc_s, *, tq, tk):
    step = pl.program_id(0)
    i = qi_ref[step]
    j = kj_ref[step]
    j_last = (i * tq + tq - 1) // tk
    rows = HEADS * tq

    @pl.when(j == 0)
    def _init():
        for h in range(HEADS):
            rs = slice(h * tq, (h + 1) * tq)
            qlat_s[rs, :] = _dot(qn_ref[:, h * QK_NOPE:(h + 1) * QK_NOPE], wn_ref[h]).astype(BF16)
            qrope_s[rs, :] = qr_ref[:, h * QK_ROPE:(h + 1) * QK_ROPE]
        m_s[...] = jnp.full(m_s.shape, -jnp.inf, F32)
        l_s[...] = jnp.zeros(l_s.shape, F32)
        acc_s[...] = jnp.zeros(acc_s.shape, F32)

    def block(masked):
        c = c_ref[...]
        s = (_dot_nt(qlat_s[...], c) + _dot_nt(qrope_s[...], kr_ref[...])) * ATTN_SCALE
        if masked:
            row = lax.broadcasted_iota(jnp.int32, (rows, tk), 0)
            col = lax.broadcasted_iota(jnp.int32, (rows, tk), 1)
            s = jnp.where(j * tk + col <= i * tq + jnp.bitwise_and(row, tq - 1), s, -jnp.inf)
        m_prev = m_s[...]
        m_new = jnp.maximum(m_prev, jnp.max(s, axis=-1, keepdims=True))
        alpha = jnp.exp(m_prev - m_new)
        p = jnp.exp(s - _rep(m_new, tk // LANES))
        l_s[...] = alpha * l_s[...] + jnp.sum(p, axis=-1, keepdims=True)
        acc_s[...] = acc_s[...] * _rep(alpha, KV_LORA // LANES) + _dot(p.astype(BF16), c)
        m_s[...] = m_new

    @pl.when(j < j_last)
    def _full():
        block(False)

    @pl.when(j == j_last)
    def _diag():
        block(True)
        for h in range(HEADS):
            rs = slice(h * tq, (h + 1) * tq)
            o_lat = (acc_s[rs, :] / _rep(l_s[rs, :], KV_LORA // LANES)).astype(BF16)
            o_ref[:, h * V_DIM:(h + 1) * V_DIM] = _dot(o_lat, wv_ref[h]).astype(BF16)


def _prompt_attn(q_nope, q_rope, w_nope, w_v, c_bf, kr_bf):
    t = q_nope.shape[0]
    tq, tk = 128, min(t, 512)
    pairs = [(i, j) for i in range(t // tq) for j in range((i * tq + tq - 1) // tk + 1)]
    qi = jnp.asarray([p[0] for p in pairs], jnp.int32)
    kj = jnp.asarray([p[1] for p in pairs], jnp.int32)
    rows = HEADS * tq
    grid_spec = pltpu.PrefetchScalarGridSpec(
        num_scalar_prefetch=2,
        grid=(len(pairs),),
        in_specs=[pl.BlockSpec((tq, HEADS * QK_NOPE), lambda s, qi, kj: (qi[s], 0)),
                  pl.BlockSpec((tq, HEADS * QK_ROPE), lambda s, qi, kj: (qi[s], 0)),
                  pl.BlockSpec((HEADS, QK_NOPE, KV_LORA), lambda s, qi, kj: (0, 0, 0)),
                  pl.BlockSpec((HEADS, KV_LORA, V_DIM), lambda s, qi, kj: (0, 0, 0)),
                  pl.BlockSpec((tk, KV_LORA), lambda s, qi, kj: (kj[s], 0)),
                  pl.BlockSpec((tk, QK_ROPE), lambda s, qi, kj: (kj[s], 0))],
        out_specs=pl.BlockSpec((tq, HEADS * V_DIM), lambda s, qi, kj: (qi[s], 0)),
        scratch_shapes=[pltpu.VMEM((rows, KV_LORA), BF16), pltpu.VMEM((rows, QK_ROPE), BF16),
                        pltpu.VMEM((rows, LANES), F32), pltpu.VMEM((rows, LANES), F32),
                        pltpu.VMEM((rows, KV_LORA), F32)],
    )
    return pl.pallas_call(
        functools.partial(_prompt_attn_kernel, tq=tq, tk=tk),
        grid_spec=grid_spec,
        out_shape=jax.ShapeDtypeStruct((t, HEADS * V_DIM), BF16),
        compiler_params=_params(1, 48),
        name="prompt_attn",
    )(qi, kj, q_nope, q_rope, w_nope, w_v, c_bf, kr_bf)


def _head_proj_in_kernel(x_ref, w_ref, o_ref):
    o_ref[0] = _dot(x_ref[...], w_ref[0]).astype(BF16)


def _q_latent_sample(q_nope, w_nope):
    n = q_nope.shape[0]
    return pl.pallas_call(
        _head_proj_in_kernel,
        grid=(HEADS,),
        in_specs=[pl.BlockSpec((n, QK_NOPE), lambda h: (0, h)),
                  pl.BlockSpec((1, QK_NOPE, KV_LORA), lambda h: (h, 0, 0))],
        out_specs=pl.BlockSpec((1, n, KV_LORA), lambda h: (h, 0, 0)),
        out_shape=jax.ShapeDtypeStruct((HEADS, n, KV_LORA), BF16),
        compiler_params=_params(1, 32),
        name="q_latent_sample",
    )(q_nope, w_nope)


def _head_proj_out_kernel(x_ref, w_ref, o_ref):
    o_ref[...] = _dot(x_ref[0], w_ref[0]).astype(BF16)


def _o_attn_sample(o_lat, w_v):
    n = o_lat.shape[1]
    return pl.pallas_call(
        _head_proj_out_kernel,
        grid=(HEADS,),
        in_specs=[pl.BlockSpec((1, n, KV_LORA), lambda h: (h, 0, 0)),
                  pl.BlockSpec((1, KV_LORA, V_DIM), lambda h: (h, 0, 0))],
        out_specs=pl.BlockSpec((n, V_DIM), lambda h: (0, h)),
        out_shape=jax.ShapeDtypeStruct((n, HEADS * V_DIM), BF16),
        compiler_params=_params(1, 32),
        name="o_attn_sample",
    )(o_lat, w_v)


def _decode_attn_kernel(pt_ref, qlat_ref, qr_ref, cnew_ref, krnew_ref, cache_c, cache_kr, o_ref,
                        cbuf, krbuf, sems, *, layer, n_seq, n_pages, group):
    n = pl.program_id(0)
    n_chunks = n_pages // group
    keys = group * PAGE

    def page_copies(seq, chunk, slot, g):
        page = pt_ref[seq * n_pages + chunk * group + g]
        return (pltpu.make_async_copy(cache_c.at[layer, page], cbuf.at[slot, g], sems.at[0, slot]),
                pltpu.make_async_copy(cache_kr.at[layer, page], krbuf.at[slot, g], sems.at[1, slot]))

    def start(seq, chunk, slot):
        for g in range(group):
            for cp in page_copies(seq, chunk, slot, g):
                cp.start()

    def wait(seq, chunk, slot):
        for g in range(group):
            for cp in page_copies(seq, chunk, slot, g):
                cp.wait()

    @pl.when(n == 0)
    def _():
        start(n, 0, 0)

    q = qlat_ref[0]
    qr = qr_ref[0]
    m = jnp.full((HEADS, 1), -jnp.inf, F32)
    l = jnp.zeros((HEADS, 1), F32)
    acc = jnp.zeros((HEADS, KV_LORA), F32)
    for ch in range(n_chunks):
        slot = ch % 2
        if ch + 1 < n_chunks:
            start(n, ch + 1, (ch + 1) % 2)
        else:
            @pl.when(n + 1 < n_seq)
            def _():
                start(n + 1, 0, 0)
        wait(n, ch, slot)
        cb = cbuf[slot].reshape(keys, KV_LORA).astype(BF16)
        krb = krbuf[slot].reshape(keys, QK_ROPE).astype(BF16)
        s = (_dot_nt(q, cb) + _dot_nt(qr, krb)) * ATTN_SCALE
        m_new = jnp.maximum(m, jnp.max(s, axis=-1, keepdims=True))
        alpha = jnp.exp(m - m_new)
        p = jnp.exp(s - m_new)
        l = alpha * l + jnp.sum(p, axis=-1, keepdims=True)
        acc = alpha * acc + _dot(p.astype(BF16), cb)
        m = m_new
    c_new = cnew_ref[0].astype(F32)
    s_new = (jnp.sum(q.astype(F32) * c_new, axis=-1, keepdims=True)
             + jnp.sum(qr.astype(F32) * krnew_ref[0].astype(F32), axis=-1, keepdims=True)) * ATTN_SCALE
    m_new = jnp.maximum(m, s_new)
    alpha = jnp.exp(m - m_new)
    p_new = jnp.exp(s_new - m_new)
    l = alpha * l + p_new
    acc = alpha * acc + p_new.astype(BF16).astype(F32) * c_new
    o_ref[0] = (acc / l).astype(BF16)


def _decode_attn(page_table, q_lat, q_rope, c_new, kr_new, cache_c, cache_kr, layer):
    n_seq, n_pages = page_table.shape
    group = min(n_pages, 16)
    while n_pages % group or (n_pages // group) % 2:
        group //= 2
    assert group >= 1 and (n_pages // group) % 2 == 0, "need an even number of page groups per sequence"
    grid_spec = pltpu.PrefetchScalarGridSpec(
        num_scalar_prefetch=1,
        grid=(n_seq,),
        in_specs=[pl.BlockSpec((1, HEADS, KV_LORA), lambda n, pt: (n, 0, 0)),
                  pl.BlockSpec((1, HEADS, QK_ROPE), lambda n, pt: (n, 0, 0)),
                  pl.BlockSpec((1, 1, KV_LORA), lambda n, pt: (n, 0, 0)),
                  pl.BlockSpec((1, 1, QK_ROPE), lambda n, pt: (n, 0, 0)),
                  pl.BlockSpec(memory_space=pl.ANY),
                  pl.BlockSpec(memory_space=pl.ANY)],
        out_specs=pl.BlockSpec((1, HEADS, KV_LORA), lambda n, pt: (n, 0, 0)),
        scratch_shapes=[pltpu.VMEM((2, group, PAGE, KV_LORA), F32),
                        pltpu.VMEM((2, group, PAGE, QK_ROPE), F32),
                        pltpu.SemaphoreType.DMA((2, 2))],
    )
    return pl.pallas_call(
        functools.partial(_decode_attn_kernel, layer=layer, n_seq=n_seq, n_pages=n_pages, group=group),
        grid_spec=grid_spec,
        out_shape=jax.ShapeDtypeStruct((n_seq, HEADS, KV_LORA), BF16),
        compiler_params=_params(1, 48),
        name="decode_attn",
    )(page_table.reshape(-1), q_lat, q_rope, c_new, kr_new, cache_c, cache_kr)


def _ssm_conv_kernel(x_ref, halo_ref, w_ref, b_ref, o_ref, xp_s):
    tm = x_ref.shape[0]
    halo = halo_ref[...]
    xp_s[0:SUBLANES, :] = jnp.where(pl.program_id(0) == 0, jnp.zeros_like(halo), halo)
    xp_s[SUBLANES:, :] = x_ref[...]
    w = w_ref[...]
    base = SUBLANES - (SSM_CONV - 1)
    y = xp_s[base:base + tm, :] * w[0:1]
    for k in range(1, SSM_CONV):
        y = y + xp_s[base + k:base + k + tm, :] * w[k:k + 1]
    o_ref[...] = _silu(y + b_ref[...])


def _ssm_conv(proj, w, b):
    t = proj.shape[0]
    tm, tn = min(t, 512), 512
    c0 = P_XBC // tn
    return pl.pallas_call(
        _ssm_conv_kernel,
        grid=(t // tm, CONV_DIM // tn),
        in_specs=[pl.BlockSpec((tm, tn), lambda i, j: (i, c0 + j)),
                  pl.BlockSpec((SUBLANES, tn), lambda i, j: (jnp.maximum(i * (tm // SUBLANES) - 1, 0), c0 + j)),
                  pl.BlockSpec((SSM_CONV, tn), lambda i, j: (0, j)),
                  pl.BlockSpec((1, tn), lambda i, j: (0, j))],
        out_specs=pl.BlockSpec((tm, tn), lambda i, j: (i, j)),
        out_shape=jax.ShapeDtypeStruct((t, CONV_DIM), F32),
        scratch_shapes=[pltpu.VMEM((tm + SUBLANES, tn), F32)],
        compiler_params=_params(2, 32),
        name="ssm_conv",
    )(proj, proj, w, b)


def _ssm_conv_step_kernel(st_ref, x_ref, w_ref, b_ref, o_ref):
    w = w_ref[...]
    y = st_ref[0] * w[0:1]
    for k in range(1, SSM_CONV - 1):
        y = y + st_ref[k] * w[k:k + 1]
    y = y + x_ref[...] * w[SSM_CONV - 1:SSM_CONV]
    o_ref[...] = _silu(y + b_ref[...])


def _ssm_conv_step(state_t, proj, w, b):
    n = proj.shape[0]
    tn = 512
    c0 = P_XBC // tn
    return pl.pallas_call(
        _ssm_conv_step_kernel,
        grid=(CONV_DIM // tn,),
        in_specs=[pl.BlockSpec((SSM_CONV - 1, n, tn), lambda j: (0, 0, j)),
                  pl.BlockSpec((n, tn), lambda j: (0, c0 + j)),
                  pl.BlockSpec((SSM_CONV, tn), lambda j: (0, j)),
                  pl.BlockSpec((1, tn), lambda j: (0, j))],
        out_specs=pl.BlockSpec((n, tn), lambda j: (0, j)),
        out_shape=jax.ShapeDtypeStruct((n, CONV_DIM), F32),
        compiler_params=_params(1, 32),
        name="ssm_conv_step",
    )(state_t, proj, w, b)


def _ssd_chunk_kernel(xbc_ref, sm_ref, dtb_ref, a_ref, dsk_ref, y_ref, hout_ref, st_s):
    c = pl.program_id(0)

    @pl.when(c == 0)
    def _():
        st_s[...] = jnp.zeros(st_s.shape, F32)

    ln = CHUNK
    dt = _softplus(sm_ref[:, LANES:2 * LANES] + dtb_ref[...])
    da = dt * a_ref[...]
    row = lax.broadcasted_iota(jnp.int32, (ln, ln), 0)
    col = lax.broadcasted_iota(jnp.int32, (ln, ln), 1)
    tri = row >= col
    cs = _dot_exact(tri.astype(F32), da)
    cs_t = cs.T
    ecs = jnp.exp(cs)
    cs_last = cs[ln - 1:ln, :]
    w_state = jnp.exp(cs_last - cs)
    e_last = jnp.exp(cs_last)
    lo_lane = lax.broadcasted_iota(jnp.int32, (ln, LANES), 1) < SSM_P
    lo_row = lax.broadcasted_iota(jnp.int32, (LANES, SSM_N), 0) < SSM_P

    def per_head(v, h0):
        return jnp.where(lo_lane, v[:, h0:h0 + 1], v[:, h0 + 1:h0 + 2])

    for g in range(SSM_GROUPS):
        bb = xbc_ref[:, D_INNER + g * SSM_N:D_INNER + (g + 1) * SSM_N].astype(BF16)
        cb = xbc_ref[:, D_INNER + GN + g * SSM_N:D_INNER + GN + (g + 1) * SSM_N].astype(BF16)
        cbt = _dot_nt(cb, bb)
        for kk in range(PAIRS_PER_GROUP):
            k = g * PAIRS_PER_GROUP + kk
            h0 = 2 * k
            sl = slice(k * LANES, (k + 1) * LANES)
            x2 = xbc_ref[:, sl]
            xd = x2 * per_head(dt, h0)
            xdb = xd.astype(BF16)

            def scores(h):
                d = cs[:, h:h + 1] - cs_t[h:h + 1, :]
                return (cbt * jnp.exp(jnp.where(tri, d, -jnp.inf))).astype(BF16)

            y_diag = jnp.where(lo_lane, _dot(scores(h0), xdb), _dot(scores(h0 + 1), xdb))
            h_prev = st_s[k]
            y_off = _dot_nt(cb, h_prev.astype(BF16)) * per_head(ecs, h0)
            y_ref[:, sl] = y_diag + y_off + dsk_ref[:, sl] * x2
            xw = xd * per_head(w_state, h0)
            st_new = _dot(xw.T.astype(BF16), bb)
            dec = jnp.where(lo_row, e_last[:, h0:h0 + 1], e_last[:, h0 + 1:h0 + 2])
            st_s[k] = h_prev * dec + st_new

    @pl.when(c == pl.num_programs(0) - 1)
    def _():
        hout_ref[...] = st_s[...]


def _ssd_chunked(xbc_act, proj, dt_bias_pad, a_pad, d_skip_cols):
    t = xbc_act.shape[0]
    return pl.pallas_call(
        _ssd_chunk_kernel,
        grid=(t // CHUNK,),
        in_specs=[pl.BlockSpec((CHUNK, CONV_DIM), lambda c: (c, 0)),
                  pl.BlockSpec((CHUNK, SMALL_W), lambda c: (c, P_SMALL // SMALL_W)),
                  pl.BlockSpec((1, LANES), lambda c: (0, 0)),
                  pl.BlockSpec((1, LANES), lambda c: (0, 0)),
                  pl.BlockSpec((1, D_INNER), lambda c: (0, 0))],
        out_specs=[pl.BlockSpec((CHUNK, D_INNER), lambda c: (c, 0)),
                   pl.BlockSpec((PAIRS, LANES, SSM_N), lambda c: (0, 0, 0))],
        out_shape=[jax.ShapeDtypeStruct((t, D_INNER), F32),
                   jax.ShapeDtypeStruct((PAIRS, LANES, SSM_N), F32)],
        scratch_shapes=[pltpu.VMEM((PAIRS, LANES, SSM_N), F32)],
        compiler_params=_params(1, 32),
        name="ssd_chunked",
    )(xbc_act, proj, dt_bias_pad, a_pad, d_skip_cols)


def _ssd_step_kernel(st_ref, x_ref, b_ref, c_ref, sm_ref, dtb_ref, a_ref, dsk_ref, y_ref, so_ref):
    k = pl.program_id(0)
    n = x_ref.shape[0]
    dt = _softplus(sm_ref[:, LANES:2 * LANES] + dtb_ref[...])
    head = lax.broadcasted_iota(jnp.int32, (LANES, LANES), 0)
    lane = lax.broadcasted_iota(jnp.int32, (LANES, LANES), 1)
    pick = (head == 2 * k + (lane >= SSM_P).astype(jnp.int32)).astype(F32)
    dt2 = _dot_exact(dt, pick)
    a2 = _dot_exact(jnp.broadcast_to(a_ref[...], (SUBLANES, LANES)), pick)[0:1]
    x2 = x_ref[...]
    xd_t = (x2 * dt2).T
    da_t = jnp.exp(dt2 * a2).T
    ycols = jnp.zeros((LANES, n), F32)
    lane_n = lax.broadcasted_iota(jnp.int32, (LANES, n), 1)
    for t in range(n):
        st = st_ref[t].reshape(2 * SSM_P, SSM_N)
        new = st * da_t[:, t:t + 1] + xd_t[:, t:t + 1] * b_ref[t:t + 1, :]
        so_ref[t] = new.reshape(2, SSM_P, SSM_N)
        ycol = jnp.sum(new * c_ref[t:t + 1, :], axis=-1, keepdims=True)
        ycols = jnp.where(lane_n == t, ycol, ycols)
    y_ref[...] = ycols.T + dsk_ref[...] * x2


def _ssd_step(state, xbc_act, proj, dt_bias_pad, a_pad, d_skip_cols):
    n = xbc_act.shape[0]
    assert n == LANES, "the per-sample SSD step transposes one (128, 128) tile of tokens"
    b0 = D_INNER // SSM_N
    c0 = (D_INNER + GN) // SSM_N
    st_spec = pl.BlockSpec((n, 2, SSM_P, SSM_N), lambda k: (0, k, 0, 0))
    return pl.pallas_call(
        _ssd_step_kernel,
        grid=(PAIRS,),
        in_specs=[st_spec,
                  pl.BlockSpec((n, LANES), lambda k: (0, k)),
                  pl.BlockSpec((n, SSM_N), lambda k: (0, b0 + k // PAIRS_PER_GROUP)),
                  pl.BlockSpec((n, SSM_N), lambda k: (0, c0 + k // PAIRS_PER_GROUP)),
                  pl.BlockSpec((n, SMALL_W), lambda k: (0, P_SMALL // SMALL_W)),
                  pl.BlockSpec((1, LANES), lambda k: (0, 0)),
                  pl.BlockSpec((1, LANES), lambda k: (0, 0)),
                  pl.BlockSpec((1, LANES), lambda k: (0, k))],
        out_specs=[pl.BlockSpec((n, LANES), lambda k: (0, k)), st_spec],
        out_shape=[jax.ShapeDtypeStruct((n, D_INNER), F32),
                   jax.ShapeDtypeStruct(state.shape, F32)],
        compiler_params=_params(1, 48),
        name="ssd_step",
    )(state, xbc_act, xbc_act, xbc_act, proj, dt_bias_pad, a_pad, d_skip_cols)


def _gate_norm_kernel(y_ref, z_ref, w_ref, o_ref):
    gw = D_INNER // SSM_GROUPS
    for g in range(SSM_GROUPS):
        sl = slice(g * gw, (g + 1) * gw)
        v = y_ref[:, sl] * _silu(z_ref[:, sl])
        o_ref[:, sl] = _rms(v, w_ref[:, sl]).astype(BF16)


def _gate_norm(y, proj, w):
    t = y.shape[0]
    tm = min(t, 512)
    return pl.pallas_call(
        _gate_norm_kernel,
        grid=(t // tm,),
        in_specs=[pl.BlockSpec((tm, D_INNER), lambda i: (i, 0)),
                  pl.BlockSpec((tm, D_INNER), lambda i: (i, P_Z // D_INNER)),
                  pl.BlockSpec((1, D_INNER), lambda i: (0, 0))],
        out_specs=pl.BlockSpec((tm, D_INNER), lambda i: (i, 0)),
        out_shape=jax.ShapeDtypeStruct((t, D_INNER), BF16),
        compiler_params=_params(1, 48),
        name="gate_norm",
    )(y, proj, w)


def _merge_kernel(oa_ref, os_ref, wa_ref, ws_ref, ga_ref, gs_ref, o_ref):
    a = _dot(oa_ref[...], wa_ref[...])
    s = _dot(os_ref[...], ws_ref[...])
    o_ref[...] = (jax.nn.sigmoid(ga_ref[...]) * a + jax.nn.sigmoid(gs_ref[...]) * s).astype(BF16)


def _merge(o_attn, o_ssm, w_attn, w_ssm, proj):
    t = o_attn.shape[0]
    tm, tn = min(t, 512), 512
    return pl.pallas_call(
        _merge_kernel,
        grid=(t // tm, D_MODEL // tn),
        in_specs=[pl.BlockSpec((tm, HEADS * V_DIM), lambda i, j: (i, 0)),
                  pl.BlockSpec((tm, D_INNER), lambda i, j: (i, 0)),
                  pl.BlockSpec((HEADS * V_DIM, tn), lambda i, j: (0, j)),
                  pl.BlockSpec((D_INNER, tn), lambda i, j: (0, j)),
                  pl.BlockSpec((tm, tn), lambda i, j: (i, P_GA // tn + j)),
                  pl.BlockSpec((tm, tn), lambda i, j: (i, P_GS // tn + j))],
        out_specs=pl.BlockSpec((tm, tn), lambda i, j: (i, j)),
        out_shape=jax.ShapeDtypeStruct((t, D_MODEL), BF16),
        compiler_params=_params(2, 48),
        name="merge",
    )(o_attn, o_ssm, w_attn, w_ssm, proj, proj)


def _out_proj_kernel(m_ref, w_ref, h_ref, npost_ref, npre_ref, h1_ref, hn_ref):
    h1 = h_ref[...] + _rms(_dot(m_ref[...], w_ref[...]), npost_ref[...])
    h1_ref[...] = h1
    hn_ref[...] = _rms(h1, npre_ref[...]).astype(BF16)


def _out_proj(merged, w_out, h, n_post, n_pre):
    t = h.shape[0]
    tm = min(t, 256)
    row = lambda i: (i, 0)
    fixed = lambda i: (0, 0)
    return pl.pallas_call(
        _out_proj_kernel,
        grid=(t // tm,),
        in_specs=[pl.BlockSpec((tm, D_MODEL), row), pl.BlockSpec((D_MODEL, D_MODEL), fixed),
                  pl.BlockSpec((tm, D_MODEL), row), pl.BlockSpec((1, D_MODEL), fixed),
                  pl.BlockSpec((1, D_MODEL), fixed)],
        out_specs=[pl.BlockSpec((tm, D_MODEL), row), pl.BlockSpec((tm, D_MODEL), row)],
        out_shape=[jax.ShapeDtypeStruct((t, D_MODEL), F32), jax.ShapeDtypeStruct((t, D_MODEL), BF16)],
        compiler_params=_params(1, 48),
        name="out_proj",
    )(merged, w_out, h, n_post, n_pre)


FFN_HALO = BF16_SUBLANES


def _ffn_kernel(x_ref, halo_ref, wg_ref, wv_ref, cwg_ref, cwv_ref, cbg_ref, cbv_ref, wd_ref, h_ref, nw_ref,
                o_ref, tg_ref, tv_ref, xe_s, u_s, acc_s):
    i, j = pl.program_id(0), pl.program_id(1)
    tm, tf = x_ref.shape[0], wg_ref.shape[1]

    @pl.when(j == 0)
    def _():
        halo = halo_ref[...]
        xe_s[0:FFN_HALO, :] = jnp.where(i == 0, jnp.zeros_like(halo), halo)
        xe_s[FFN_HALO:, :] = x_ref[...]

    xe = xe_s[...]
    u_s[:, 0:tf] = _dot(xe, wg_ref[...])
    u_s[:, tf:] = _dot(xe, wv_ref[...])
    base = FFN_HALO - (FFN_CONV - 1)

    def conv(lo, w_ref, b_ref):
        w = w_ref[...]
        y = u_s[base:base + tm, lo:lo + tf] * w[0:1]
        for k in range(1, FFN_CONV):
            y = y + u_s[base + k:base + k + tm, lo:lo + tf] * w[k:k + 1]
        return y + b_ref[...]

    f = (_gelu_tanh(conv(0, cwg_ref, cbg_ref)) * conv(tf, cwv_ref, cbv_ref)).astype(BF16)
    d = _dot(f, wd_ref[...])

    @pl.when(j == 0)
    def _():
        acc_s[...] = d

    @pl.when(j > 0)
    def _():
        acc_s[...] += d

    tail = FFN_HALO + tm - SUBLANES
    tg_ref[...] = u_s[tail:tail + SUBLANES, 0:tf]
    tv_ref[...] = u_s[tail:tail + SUBLANES, tf:]

    @pl.when(j == pl.num_programs(1) - 1)
    def _():
        o_ref[...] = h_ref[...] + _rms(acc_s[...], nw_ref[...])


def _ffn(hn, h, w_up, conv_w, conv_b, w_down, nw):
    t = h.shape[0]
    tm, tf = min(t, 512), 512
    nj = D_FF // tf
    gate = lambda i, j: (0, j)
    value = lambda i, j: (0, nj + j)
    row = lambda i, j: (i, 0)
    return pl.pallas_call(
        _ffn_kernel,
        grid=(t // tm, nj),
        in_specs=[pl.BlockSpec((tm, D_MODEL), row),
                  pl.BlockSpec((FFN_HALO, D_MODEL), lambda i, j: (jnp.maximum(i * (tm // FFN_HALO) - 1, 0), 0)),
                  pl.BlockSpec((D_MODEL, tf), gate), pl.BlockSpec((D_MODEL, tf), value),
                  pl.BlockSpec((FFN_CONV, tf), gate), pl.BlockSpec((FFN_CONV, tf), value),
                  pl.BlockSpec((1, tf), gate), pl.BlockSpec((1, tf), value),
                  pl.BlockSpec((tf, D_MODEL), lambda i, j: (j, 0)),
                  pl.BlockSpec((tm, D_MODEL), row),
                  pl.BlockSpec((1, D_MODEL), lambda i, j: (0, 0))],
        out_specs=[pl.BlockSpec((tm, D_MODEL), row),
                   pl.BlockSpec((SUBLANES, tf), lambda i, j: (i, j)),
                   pl.BlockSpec((SUBLANES, tf), lambda i, j: (i, j))],
        out_shape=[jax.ShapeDtypeStruct((t, D_MODEL), F32),
                   jax.ShapeDtypeStruct((t // tm * SUBLANES, D_FF), F32),
                   jax.ShapeDtypeStruct((t // tm * SUBLANES, D_FF), F32)],
        scratch_shapes=[pltpu.VMEM((tm + FFN_HALO, D_MODEL), BF16),
                        pltpu.VMEM((tm + FFN_HALO, 2 * tf), F32),
                        pltpu.VMEM((tm, D_MODEL), F32)],
        compiler_params=_params(2, 56),
        name="ffn",
    )(hn, hn, w_up, w_up, conv_w, conv_w, conv_b, conv_b, w_down, h, nw)


def _ffn_step_kernel(x_ref, wg_ref, wv_ref, sg_ref, sv_ref, cwg_ref, cwv_ref, cbg_ref, cbv_ref, wd_ref, h_ref,
                     nw_ref, o_ref, ug_ref, uv_ref, acc_s):
    j = pl.program_id(0)
    x = x_ref[...]

    def conv(u, s_ref, w_ref, b_ref):
        w = w_ref[...]
        y = s_ref[0] * w[0:1]
        for k in range(1, FFN_CONV - 1):
            y = y + s_ref[k] * w[k:k + 1]
        return y + u * w[FFN_CONV - 1:FFN_CONV] + b_ref[...]

    ug = _dot(x, wg_ref[...])
    uv = _dot(x, wv_ref[...])
    ug_ref[...] = ug
    uv_ref[...] = uv
    f = (_gelu_tanh(conv(ug, sg_ref, cwg_ref, cbg_ref)) * conv(uv, sv_ref, cwv_ref, cbv_ref)).astype(BF16)
    d = _dot(f, wd_ref[...])

    @pl.when(j == 0)
    def _():
        acc_s[...] = d

    @pl.when(j > 0)
    def _():
        acc_s[...] += d

    @pl.when(j == pl.num_programs(0) - 1)
    def _():
        o_ref[...] = h_ref[...] + _rms(acc_s[...], nw_ref[...])


def _ffn_step(hn, h, state_t, w_up, conv_w, conv_b, w_down, nw):
    n = h.shape[0]
    tf = 512
    nj = D_FF // tf
    gate = lambda j: (0, j)
    value = lambda j: (0, nj + j)
    fixed = lambda j: (0, 0)
    return pl.pallas_call(
        _ffn_step_kernel,
        grid=(nj,),
        in_specs=[pl.BlockSpec((n, D_MODEL), fixed),
                  pl.BlockSpec((D_MODEL, tf), gate), pl.BlockSpec((D_MODEL, tf), value),
                  pl.BlockSpec((FFN_CONV - 1, n, tf), lambda j: (0, 0, j)),
                  pl.BlockSpec((FFN_CONV - 1, n, tf), lambda j: (0, 0, nj + j)),
                  pl.BlockSpec((FFN_CONV, tf), gate), pl.BlockSpec((FFN_CONV, tf), value),
                  pl.BlockSpec((1, tf), gate), pl.BlockSpec((1, tf), value),
                  pl.BlockSpec((tf, D_MODEL), lambda j: (j, 0)),
                  pl.BlockSpec((n, D_MODEL), fixed),
                  pl.BlockSpec((1, D_MODEL), fixed)],
        out_specs=[pl.BlockSpec((n, D_MODEL), fixed), pl.BlockSpec((n, tf), gate), pl.BlockSpec((n, tf), gate)],
        out_shape=[jax.ShapeDtypeStruct((n, D_MODEL), F32),
                   jax.ShapeDtypeStruct((n, D_FF), F32), jax.ShapeDtypeStruct((n, D_FF), F32)],
        scratch_shapes=[pltpu.VMEM((n, D_MODEL), F32)],
        compiler_params=_params(1, 32),
        name="ffn_step",
    )(hn, w_up, w_up, state_t, state_t, conv_w, conv_w, conv_b, conv_b, w_down, h, nw)


def _ple_kernel(h_ref, pe_ref, wg_ref, wp_ref, nw_ref, o_ref):
    h = h_ref[...]
    gate = jax.nn.sigmoid(_dot(h.astype(BF16), wg_ref[...]))
    o_ref[...] = h + _rms(gate * _dot(pe_ref[...].astype(BF16), wp_ref[...]), nw_ref[...])


def _ple(h, pe, w_gate, w_proj, nw):
    t = h.shape[0]
    tm = min(t, 256)
    row = lambda i: (i, 0)
    fixed = lambda i: (0, 0)
    return pl.pallas_call(
        _ple_kernel,
        grid=(t // tm,),
        in_specs=[pl.BlockSpec((tm, D_MODEL), row), pl.BlockSpec((tm, PLE_DIM), row),
                  pl.BlockSpec((D_MODEL, D_MODEL), fixed), pl.BlockSpec((PLE_DIM, D_MODEL), fixed),
                  pl.BlockSpec((1, D_MODEL), fixed)],
        out_specs=pl.BlockSpec((tm, D_MODEL), row),
        out_shape=jax.ShapeDtypeStruct((t, D_MODEL), F32),
        compiler_params=_params(1, 48),
        name="ple",
    )(h, pe, w_gate, w_proj, nw)


def _rot_half_cols(w):
    half = QK_ROPE // 2
    return jnp.concatenate([-w[..., half:], w[..., :half]], axis=-1)


def _prepare_weights(w):
    splits = np.cumsum([0, Q_LORA, KV_LORA, QK_ROPE, D_INNER, CONV_DIM, SSM_HEADS, D_MODEL, D_MODEL])
    q_c, kv_c, k_r, z, xbc, dt, g_a, g_s = [w['w_in'][:, a:b] for a, b in zip(splits[:-1], splits[1:])]
    pad = jnp.zeros((D_MODEL, SMALL_W - 2 * QK_ROPE - SSM_HEADS), F32)
    packed = jnp.concatenate([z, g_a, g_s, xbc, kv_c, k_r, _rot_half_cols(k_r), dt, pad, q_c], axis=1)
    wq = w['w_q_up'].reshape(Q_LORA, HEADS, QK_NOPE + QK_ROPE)
    wq_rope = wq[:, :, QK_NOPE:]
    wkv = w['w_kv_up'].reshape(KV_LORA, HEADS, QK_NOPE + V_DIM)
    lane_pad = LANES - SSM_HEADS
    return dict(
        w_in=packed.astype(BF16),
        wq_nope=wq[:, :, :QK_NOPE].reshape(Q_LORA, HEADS * QK_NOPE).astype(BF16),
        wq_rope=wq_rope.reshape(Q_LORA, HEADS * QK_ROPE).astype(BF16),
        wq_rope_rot=_rot_half_cols(wq_rope).reshape(Q_LORA, HEADS * QK_ROPE).astype(BF16),
        w_nope=wkv[:, :, :QK_NOPE].transpose(1, 2, 0).astype(BF16),
        w_v=wkv[:, :, QK_NOPE:].transpose(1, 0, 2).astype(BF16),
        w_branch_attn=w['w_branch_attn'].astype(BF16),
        w_branch_ssm=w['w_branch_ssm'].astype(BF16),
        w_out=w['w_out'].astype(BF16),
        w_ffn_up=w['w_ffn_up'].astype(BF16),
        w_ffn_down=w['w_ffn_down'].astype(BF16),
        w_ple_gate=w['w_ple_gate'].astype(BF16),
        w_ple_proj=w['w_ple_proj'].astype(BF16),
        dt_bias=jnp.pad(w['dt_bias'], (0, lane_pad))[None, :],
        a_neg=jnp.pad(-jnp.exp(w['a_log']), (0, lane_pad))[None, :],
        d_skip=jnp.repeat(w['d_skip'], SSM_P)[None, :],
    )


def _rope_tables(pos):
    half = QK_ROPE // 2
    inv = ROPE_THETA ** (-jnp.arange(half, dtype=F32) / half)
    ang = pos[:, None] * inv[None, :]
    cos, sin = jnp.cos(ang), jnp.sin(ang)
    cos64, sin64 = jnp.concatenate([cos, cos], axis=1), jnp.concatenate([sin, sin], axis=1)
    return cos64, sin64, jnp.concatenate([cos64, cos64], axis=1), jnp.concatenate([sin64, sin64], axis=1)


def _vec(v):
    return v[None, :]


def _mixer_front(h, pos, w, pw):
    proj = _in_proj(h, _vec(w['norm_pre_mix']), pw['w_in'])
    cos64, sin64, cos128, sin128 = _rope_tables(pos)
    c, c_bf, kr, kr_bf = _kv_prep(proj, cos64, sin64, _vec(w['norm_kv']))
    q_nope, q_rope = _q_prep(proj, _vec(w['norm_q']), pw['wq_nope'], pw['wq_rope'], pw['wq_rope_rot'],
                             cos128, sin128)
    return proj, c, c_bf, kr, kr_bf, q_nope, q_rope


def _mixer_back(h, proj, o_attn, y_ssm, w, pw):
    o_ssm = _gate_norm(y_ssm, proj, _vec(w['norm_ssm']))
    merged = _merge(o_attn, o_ssm, pw['w_branch_attn'], pw['w_branch_ssm'], proj)
    return _out_proj(merged, pw['w_out'], h, _vec(w['norm_post_mix']), _vec(w['norm_pre_ffn']))


def _prompt_layer(h, pe, w, pw):
    t = h.shape[0]
    pos = jnp.arange(t, dtype=F32)
    proj, c, c_bf, kr, kr_bf, q_nope, q_rope = _mixer_front(h, pos, w, pw)
    o_attn = _prompt_attn(q_nope, q_rope, pw['w_nope'], pw['w_v'], c_bf, kr_bf)
    xbc_act = _ssm_conv(proj, w['conv_ssm_w'], _vec(w['conv_ssm_b']))
    y_ssm, st = _ssd_chunked(xbc_act, proj, pw['dt_bias'], pw['a_neg'], pw['d_skip'])
    h1, hn1 = _mixer_back(h, proj, o_attn, y_ssm, w, pw)
    h2, tail_g, tail_v = _ffn(hn1, h1, pw['w_ffn_up'], w['conv_ffn_w'], _vec(w['conv_ffn_b']),
                              pw['w_ffn_down'], _vec(w['norm_post_ffn']))
    h3 = _ple(h2, pe, pw['w_ple_gate'], pw['w_ple_proj'], _vec(w['norm_ple']))
    ssm_conv_new = proj[t - (SSM_CONV - 1):, P_XBC:P_XBC + CONV_DIM]
    last = tail_g.shape[0] - (FFN_CONV - 1)
    ffn_conv_new = jnp.concatenate([tail_g[last:], tail_v[last:]], axis=1)
    return h3, (c, kr, st.reshape(SSM_HEADS, SSM_P, SSM_N), ssm_conv_new, ffn_conv_new)


def _sample_layer(h, pe, past, w, pw, cache_c, cache_kr, layer, page_table, ssm_state, ssm_conv_state,
                  ffn_conv_state):
    n = h.shape[0]
    pos = jnp.full((n,), past, F32)
    proj, c, c_bf, kr, kr_bf, q_nope, q_rope = _mixer_front(h, pos, w, pw)
    q_lat = _q_latent_sample(q_nope, pw['w_nope']).transpose(1, 0, 2)
    o_lat = _decode_attn(page_table, q_lat, q_rope.reshape(n, HEADS, QK_ROPE), c_bf[:, None, :],
                         kr_bf[:, None, :], cache_c, cache_kr, layer)
    o_attn = _o_attn_sample(o_lat.transpose(1, 0, 2), pw['w_v'])
    xbc_act = _ssm_conv_step(ssm_conv_state.transpose(1, 0, 2), proj, w['conv_ssm_w'], _vec(w['conv_ssm_b']))
    y_ssm, st = _ssd_step(ssm_state, xbc_act, proj, pw['dt_bias'], pw['a_neg'], pw['d_skip'])
    h1, hn1 = _mixer_back(h, proj, o_attn, y_ssm, w, pw)
    h2, u_g, u_v = _ffn_step(hn1, h1, ffn_conv_state.transpose(1, 0, 2), pw['w_ffn_up'], w['conv_ffn_w'],
                             _vec(w['conv_ffn_b']), pw['w_ffn_down'], _vec(w['norm_post_ffn']))
    h3 = _ple(h2, pe, pw['w_ple_gate'], pw['w_ple_proj'], _vec(w['norm_ple']))
    xbc_raw = proj[:, P_XBC:P_XBC + CONV_DIM]
    ssm_conv_new = jnp.concatenate([ssm_conv_state[:, 1:], xbc_raw[:, None, :]], axis=1)
    u = jnp.concatenate([u_g, u_v], axis=1)
    ffn_conv_new = jnp.concatenate([ffn_conv_state[:, 1:], u[:, None, :]], axis=1)
    return h3, (c[:, None, :], kr[:, None, :], st, ssm_conv_new, ffn_conv_new)


def kernel(x_prompt, x_sample, cache_kv_latent, cache_k_rope, state_ssm, state_ssm_conv, state_ffn_conv, page_table, p_prompt, p_sample, norm_pre_mix, w_in, norm_q, w_q_up, norm_kv, w_kv_up, w_branch_attn, conv_ssm_w, conv_ssm_b, dt_bias, a_log, d_skip, norm_ssm, w_branch_ssm, w_out, norm_post_mix, norm_pre_ffn, w_ffn_up, conv_ffn_w, conv_ffn_b, w_ffn_down, norm_post_ffn, w_ple_gate, w_ple_proj, norm_ple):
    assert x_prompt.shape[0] == 1, "one prompt sequence"
    assert x_sample.shape[1] == 1, "one new token per sample sequence"
    depth = w_in.shape[0]
    past = page_table.shape[1] * PAGE
    weights = dict(norm_pre_mix=norm_pre_mix, w_in=w_in, norm_q=norm_q, w_q_up=w_q_up, norm_kv=norm_kv,
                   w_kv_up=w_kv_up, w_branch_attn=w_branch_attn, conv_ssm_w=conv_ssm_w, conv_ssm_b=conv_ssm_b,
                   dt_bias=dt_bias, a_log=a_log, d_skip=d_skip, norm_ssm=norm_ssm, w_branch_ssm=w_branch_ssm,
                   w_out=w_out, norm_post_mix=norm_post_mix, norm_pre_ffn=norm_pre_ffn, w_ffn_up=w_ffn_up,
                   conv_ffn_w=conv_ffn_w, conv_ffn_b=conv_ffn_b, w_ffn_down=w_ffn_down,
                   norm_post_ffn=norm_post_ffn, w_ple_gate=w_ple_gate, w_ple_proj=w_ple_proj, norm_ple=norm_ple)
    hp, hs = x_prompt[0], x_sample[:, 0]
    outs_p, outs_s = [], []
    for d in range(depth):
        w = {k: v[d] for k, v in weights.items()}
        pw = _prepare_weights(w)
        hp, st_p = _prompt_layer(hp, p_prompt[d, 0], w, pw)
        hs, st_s = _sample_layer(hs, p_sample[d, :, 0], past, w, pw, cache_kv_latent, cache_k_rope, d,
                                 page_table, state_ssm[d], state_ssm_conv[d], state_ffn_conv[d])
        outs_p.append(st_p)
        outs_s.append(st_s)
    stack_p = [jnp.stack([o[k] for o in outs_p])[:, None] for k in range(5)]
    stack_s = [jnp.stack([o[k] for o in outs_s]) for k in range(5)]
    return (hp[None], hs[:, None], *stack_p, *stack_s)
```

```python
import functools

import jax
import jax.numpy as jnp
import numpy as np
from jax import lax
from jax.experimental import pallas as pl
from jax.experimental.pallas import tpu as pltpu

F32, BF16 = jnp.float32, jnp.bfloat16
HIGHEST = lax.Precision.HIGHEST

LANES = 128
SUBLANES = 8
BF16_SUBLANES = 16
V7X_VMEM_BYTES = 64 * 1024 * 1024

D_MODEL = 2048
PAGE = 128
HEADS = 16
QK_NOPE = 128
QK_ROPE = 64
V_DIM = 128
Q_LORA = 768
KV_LORA = 512
ROPE_THETA = 10000.0
ATTN_SCALE = (QK_NOPE + QK_ROPE) ** -0.5
SOFTMAX_LOG2_SCALE = ATTN_SCALE * float(np.log2(np.e))
SSM_HEADS = 32
SSM_P = 64
D_INNER = SSM_HEADS * SSM_P
SSM_GROUPS = 4
SSM_N = 128
SSM_CONV = 4
GN = SSM_GROUPS * SSM_N
CONV_DIM = D_INNER + 2 * GN
CHUNK = 128
PAIRS = SSM_HEADS // 2
PAIRS_PER_GROUP = PAIRS // SSM_GROUPS
D_FF = 4 * D_MODEL
FFN_CONV = 3
PLE_DIM = 256
EPS = 1e-6

P_Z = 0
P_GA = P_Z + D_INNER
P_GS = P_GA + D_MODEL
P_XBC = P_GS + D_MODEL
P_KV = P_XBC + CONV_DIM
P_SMALL = P_KV + KV_LORA
P_Q = P_SMALL + 2 * LANES
P_COLS = P_Q + Q_LORA
SMALL_W = 2 * LANES


def _params(n_axes, vmem_mib):
    return pltpu.CompilerParams(dimension_semantics=("arbitrary",) * n_axes,
                                vmem_limit_bytes=vmem_mib * 1024 * 1024)


def _rms(x, w):
    return x * lax.rsqrt(jnp.mean(x * x, axis=-1, keepdims=True) + EPS) * w


def _dot(a, b):
    return jnp.dot(a, b, preferred_element_type=F32)


def _dot_nt(a, b):
    return lax.dot_general(a, b, (((1,), (1,)), ((), ())), preferred_element_type=F32)


def _dot_exact(a, b):
    return jnp.dot(a, b, precision=HIGHEST, preferred_element_type=F32)


def _rep(x, n):
    return x if n == 1 else jnp.concatenate([x] * n, axis=1)


def _silu(x):
    return x * jax.nn.sigmoid(x)


def _softplus(x):
    return jnp.maximum(x, 0.0) + jnp.log1p(jnp.exp(-jnp.abs(x)))


def _gelu_tanh(x):
    return x * (0.5 * (1.0 + jnp.tanh(np.sqrt(2.0 / np.pi).astype(np.float32) * (x + 0.044715 * (x * x * x)))))


def _in_proj_kernel(x_ref, nw_ref, w_ref, o_ref, xn_ref):
    @pl.when(pl.program_id(1) == 0)
    def _():
        xn_ref[...] = _rms(x_ref[...], nw_ref[...]).astype(BF16)

    o_ref[...] = _dot(xn_ref[...], w_ref[...])


def _in_proj(x, nw, w_packed):
    t = x.shape[0]
    tm, tn = min(t, 1024), 512
    return pl.pallas_call(
        _in_proj_kernel,
        grid=(t // tm, P_COLS // tn),
        in_specs=[pl.BlockSpec((tm, D_MODEL), lambda i, j: (i, 0)),
                  pl.BlockSpec((1, D_MODEL), lambda i, j: (0, 0)),
                  pl.BlockSpec((D_MODEL, tn), lambda i, j: (0, j))],
        out_specs=pl.BlockSpec((tm, tn), lambda i, j: (i, j)),
        out_shape=jax.ShapeDtypeStruct((t, P_COLS), F32),
        scratch_shapes=[pltpu.VMEM((tm, D_MODEL), BF16)],
        compiler_params=_params(2, 48),
        name="in_proj",
    )(x, nw, w_packed)


def _kv_prep_kernel(kv_ref, sm_ref, cos_ref, sin_ref, nw_ref, c_ref, cbf_ref, kr_ref, krbf_ref):
    c = _rms(kv_ref[...], nw_ref[...])
    c_ref[...] = c
    cbf_ref[...] = c.astype(BF16)
    sm = sm_ref[...]
    kr = sm[:, 0:QK_ROPE] * cos_ref[...] + sm[:, QK_ROPE:2 * QK_ROPE] * sin_ref[...]
    kr_ref[...] = kr
    krbf_ref[...] = kr.astype(BF16)


def _kv_prep(proj, cos64, sin64, nw):
    t = proj.shape[0]
    tm = min(t, 512)
    return pl.pallas_call(
        _kv_prep_kernel,
        grid=(t // tm,),
        in_specs=[pl.BlockSpec((tm, KV_LORA), lambda i: (i, P_KV // KV_LORA)),
                  pl.BlockSpec((tm, SMALL_W), lambda i: (i, P_SMALL // SMALL_W)),
                  pl.BlockSpec((tm, QK_ROPE), lambda i: (i, 0)),
                  pl.BlockSpec((tm, QK_ROPE), lambda i: (i, 0)),
                  pl.BlockSpec((1, KV_LORA), lambda i: (0, 0))],
        out_specs=[pl.BlockSpec((tm, KV_LORA), lambda i: (i, 0)),
                   pl.BlockSpec((tm, KV_LORA), lambda i: (i, 0)),
                   pl.BlockSpec((tm, QK_ROPE), lambda i: (i, 0)),
                   pl.BlockSpec((tm, QK_ROPE), lambda i: (i, 0))],
        out_shape=[jax.ShapeDtypeStruct((t, KV_LORA), F32), jax.ShapeDtypeStruct((t, KV_LORA), BF16),
                   jax.ShapeDtypeStruct((t, QK_ROPE), F32), jax.ShapeDtypeStruct((t, QK_ROPE), BF16)],
        compiler_params=_params(1, 32),
        name="kv_prep",
    )(proj, proj, cos64, sin64, nw)


def _q_prep_kernel(q_ref, nw_ref, wn_ref, wr_ref, wrr_ref, cos_ref, sin_ref, qn_ref, qr_ref):
    qn = _rms(q_ref[...], nw_ref[...]).astype(BF16)
    qn_ref[...] = _dot(qn, wn_ref[...]).astype(BF16)
    r = _dot(qn, wr_ref[...])
    rr = _dot(qn, wrr_ref[...])
    cos, sin = cos_ref[...], sin_ref[...]
    for k in range(HEADS * QK_ROPE // LANES):
        sl = slice(k * LANES, (k + 1) * LANES)
        qr_ref[:, sl] = (r[:, sl] * cos + rr[:, sl] * sin).astype(BF16)


def _q_prep(proj, nw, wq_nope, wq_rope, wq_rope_rot, cos128, sin128):
    t = proj.shape[0]
    tm = min(t, 512)
    n_nope, n_rope = HEADS * QK_NOPE, HEADS * QK_ROPE
    return pl.pallas_call(
        _q_prep_kernel,
        grid=(t // tm,),
        in_specs=[pl.BlockSpec((tm, Q_LORA), lambda i: (i, P_Q // Q_LORA)),
                  pl.BlockSpec((1, Q_LORA), lambda i: (0, 0)),
                  pl.BlockSpec((Q_LORA, n_nope), lambda i: (0, 0)),
                  pl.BlockSpec((Q_LORA, n_rope), lambda i: (0, 0)),
                  pl.BlockSpec((Q_LORA, n_rope), lambda i: (0, 0)),
                  pl.BlockSpec((tm, LANES), lambda i: (i, 0)),
                  pl.BlockSpec((tm, LANES), lambda i: (i, 0))],
        out_specs=[pl.BlockSpec((tm, n_nope), lambda i: (i, 0)),
                   pl.BlockSpec((tm, n_rope), lambda i: (i, 0))],
        out_shape=[jax.ShapeDtypeStruct((t, n_nope), BF16), jax.ShapeDtypeStruct((t, n_rope), BF16)],
        compiler_params=_params(1, 48),
        name="q_prep",
    )(proj, nw, wq_nope, wq_rope, wq_rope_rot, cos128, sin128)


def _prompt_attn_kernel(qi_ref, kj_ref, qn_ref, qr_ref, wn_ref, wv_ref, c_ref, kr_ref, o_ref,
                        qlat_s, qrope_s, m_s, l_s, acc_s, *, tq):
    step = pl.program_id(0)
    i = qi_ref[step]
    j = kj_ref[step]
    j_last = i // 4
    r = i % 4

    @pl.when(j == 0)
    def _init():
        for h in range(HEADS):
            rs = slice(h * tq, (h + 1) * tq)
            qlat_s[rs, :] = _dot(qn_ref[:, h * QK_NOPE:(h + 1) * QK_NOPE], wn_ref[h]).astype(BF16)
            qrope_s[rs, :] = qr_ref[:, h * QK_ROPE:(h + 1) * QK_ROPE]
        m_s[...] = jnp.full(m_s.shape, -jnp.inf, F32)
        l_s[...] = jnp.zeros(l_s.shape, F32)
        acc_s[...] = jnp.zeros(acc_s.shape, F32)

    def scores(k0, nk):
        return (_dot_nt(qlat_s[...], c_ref[k0:k0 + nk, :])
                + _dot_nt(qrope_s[...], kr_ref[k0:k0 + nk, :])) * SOFTMAX_LOG2_SCALE

    def update(s, k0, nk, limit=None):
        if limit is not None:
            visible = (lax.broadcasted_iota(jnp.int32, (1, tq, nk), 2)
                       <= lax.broadcasted_iota(jnp.int32, (1, tq, nk), 1) + limit)
            s = jnp.where(visible, s.reshape(HEADS, tq, nk), -jnp.inf).reshape(HEADS * tq, nk)
        m_prev = m_s[...]
        m_new = jnp.maximum(m_prev, jnp.max(s, axis=-1, keepdims=True))
        alpha = jnp.exp2(m_prev - m_new)
        p = jnp.exp2(s - _rep(m_new, nk // LANES))
        l_s[...] = alpha * l_s[...] + jnp.sum(p, axis=-1, keepdims=True)
        acc_s[...] = acc_s[...] * _rep(alpha, KV_LORA // LANES) + _dot(p.astype(BF16), c_ref[k0:k0 + nk, :])
        m_s[...] = m_new

    @pl.when(j < j_last)
    def _below_diagonal():
        s_a = scores(0, 2 * tq)
        s_b = scores(2 * tq, 2 * tq)
        update(s_a, 0, 2 * tq)
        update(s_b, 2 * tq, 2 * tq)

    for quarter in range(4):
        @pl.when(jnp.logical_and(j == j_last, quarter <= r))
        def _near_diagonal():
            update(scores(quarter * tq, tq), quarter * tq, tq, limit=(r - quarter) * tq)

    @pl.when(j == j_last)
    def _finish():
        for h in range(HEADS):
            rs = slice(h * tq, (h + 1) * tq)
            o_lat = (acc_s[rs, :] / _rep(l_s[rs, :], KV_LORA // LANES)).astype(BF16)
            o_ref[:, h * V_DIM:(h + 1) * V_DIM] = _dot(o_lat, wv_ref[h]).astype(BF16)


def _prompt_attn(q_nope, q_rope, w_nope, w_v, c_bf, kr_bf):
    t = q_nope.shape[0]
    tq = 256
    tk = 4 * tq
    assert t % tk == 0, "prompt length must be a multiple of the key block"
    pairs = [(i, j) for i in range(t // tq) for j in range(i // 4 + 1)]
    qi = jnp.asarray([p[0] for p in pairs], jnp.int32)
    kj = jnp.asarray([p[1] for p in pairs], jnp.int32)
    rows = HEADS * tq
    grid_spec = pltpu.PrefetchScalarGridSpec(
        num_scalar_prefetch=2,
        grid=(len(pairs),),
        in_specs=[pl.BlockSpec((tq, HEADS * QK_NOPE), lambda s, qi, kj: (qi[s], 0)),
                  pl.BlockSpec((tq, HEADS * QK_ROPE), lambda s, qi, kj: (qi[s], 0)),
                  pl.BlockSpec((HEADS, QK_NOPE, KV_LORA), lambda s, qi, kj: (0, 0, 0)),
                  pl.BlockSpec((HEADS, KV_LORA, V_DIM), lambda s, qi, kj: (0, 0, 0)),
                  pl.BlockSpec((tk, KV_LORA), lambda s, qi, kj: (kj[s], 0)),
                  pl.BlockSpec((tk, QK_ROPE), lambda s, qi, kj: (kj[s], 0))],
        out_specs=pl.BlockSpec((tq, HEADS * V_DIM), lambda s, qi, kj: (qi[s], 0)),
        scratch_shapes=[pltpu.VMEM((rows, KV_LORA), BF16), pltpu.VMEM((rows, QK_ROPE), BF16),
                        pltpu.VMEM((rows, LANES), F32), pltpu.VMEM((rows, LANES), F32),
                        pltpu.VMEM((rows, KV_LORA), F32)],
    )
    return pl.pallas_call(
        functools.partial(_prompt_attn_kernel, tq=tq),
        grid_spec=grid_spec,
        out_shape=jax.ShapeDtypeStruct((t, HEADS * V_DIM), BF16),
        compiler_params=_params(1, 56),
        name="prompt_attn",
    )(qi, kj, q_nope, q_rope, w_nope, w_v, c_bf, kr_bf)


def _head_proj_in_kernel(x_ref, w_ref, o_ref):
    o_ref[0] = _dot(x_ref[...], w_ref[0]).astype(BF16)


def _q_latent_sample(q_nope, w_nope):
    n = q_nope.shape[0]
    return pl.pallas_call(
        _head_proj_in_kernel,
        grid=(HEADS,),
        in_specs=[pl.BlockSpec((n, QK_NOPE), lambda h: (0, h)),
                  pl.BlockSpec((1, QK_NOPE, KV_LORA), lambda h: (h, 0, 0))],
        out_specs=pl.BlockSpec((1, n, KV_LORA), lambda h: (h, 0, 0)),
        out_shape=jax.ShapeDtypeStruct((HEADS, n, KV_LORA), BF16),
        compiler_params=_params(1, 32),
        name="q_latent_sample",
    )(q_nope, w_nope)


def _head_proj_out_kernel(x_ref, w_ref, o_ref):
    o_ref[...] = _dot(x_ref[0], w_ref[0]).astype(BF16)


def _o_attn_sample(o_lat, w_v):
    n = o_lat.shape[1]
    return pl.pallas_call(
        _head_proj_out_kernel,
        grid=(HEADS,),
        in_specs=[pl.BlockSpec((1, n, KV_LORA), lambda h: (h, 0, 0)),
                  pl.BlockSpec((1, KV_LORA, V_DIM), lambda h: (h, 0, 0))],
        out_specs=pl.BlockSpec((n, V_DIM), lambda h: (0, h)),
        out_shape=jax.ShapeDtypeStruct((n, HEADS * V_DIM), BF16),
        compiler_params=_params(1, 32),
        name="o_attn_sample",
    )(o_lat, w_v)


DECODE_SLOTS = 4


def _decode_attn_kernel(pt_ref, qlat_ref, qr_ref, cnew_ref, krnew_ref, cache_c, cache_krt, o_ref,
                        cbuf, krbuf, sems, *, layer, n_seq, n_pages, group):
    n = pl.program_id(0)
    n_chunks = n_pages // group
    keys = group * PAGE
    ahead = DECODE_SLOTS - 1

    def page_copies(seq, chunk, g):
        slot = chunk % DECODE_SLOTS
        page = pt_ref[seq * n_pages + chunk * group + g]
        return (pltpu.make_async_copy(cache_c.at[layer, page], cbuf.at[slot, g], sems.at[0, slot]),
                pltpu.make_async_copy(cache_krt.at[layer, page], krbuf.at[slot, :, pl.ds(g * PAGE, PAGE)],
                                      sems.at[1, slot]))

    def start(seq, chunk):
        for g in range(group):
            for cp in page_copies(seq, chunk, g):
                cp.start()

    def wait(seq, chunk):
        for g in range(group):
            for cp in page_copies(seq, chunk, g):
                cp.wait()

    @pl.when(n == 0)
    def _():
        for ch in range(ahead):
            start(n, ch)

    q = qlat_ref[0]
    qr = qr_ref[0]
    m = jnp.full((HEADS, 1), -jnp.inf, F32)
    l = jnp.zeros((HEADS, 1), F32)
    acc = jnp.zeros((HEADS, KV_LORA), F32)
    for ch in range(n_chunks):
        nxt = ch + ahead
        if nxt < n_chunks:
            start(n, nxt)
        else:
            @pl.when(n + 1 < n_seq)
            def _():
                start(n + 1, nxt - n_chunks)
        wait(n, ch)
        slot = ch % DECODE_SLOTS
        cb = cbuf[slot].reshape(keys, KV_LORA).astype(BF16)
        krt = krbuf[slot].astype(BF16)
        s = (_dot_nt(q, cb) + _dot(qr, krt)) * SOFTMAX_LOG2_SCALE
        m_new = jnp.maximum(m, jnp.max(s, axis=-1, keepdims=True))
        alpha = jnp.exp2(m - m_new)
        p = jnp.exp2(s - m_new)
        l = alpha * l + jnp.sum(p, axis=-1, keepdims=True)
        acc = alpha * acc + _dot(p.astype(BF16), cb)
        m = m_new
    c_new = cnew_ref[0].astype(F32)
    s_new = (jnp.sum(q.astype(F32) * c_new, axis=-1, keepdims=True)
             + jnp.sum(qr.astype(F32) * krnew_ref[0].astype(F32), axis=-1, keepdims=True)) * SOFTMAX_LOG2_SCALE
    m_new = jnp.maximum(m, s_new)
    alpha = jnp.exp2(m - m_new)
    p_new = jnp.exp2(s_new - m_new)
    l = alpha * l + p_new
    acc = alpha * acc + p_new.astype(BF16).astype(F32) * c_new
    o_ref[0] = (acc / l).astype(BF16)


def _decode_attn(page_table, q_lat, q_rope, c_new, kr_new, cache_c, cache_krt, layer):
    n_seq, n_pages = page_table.shape
    group = 8
    while n_pages % (group * DECODE_SLOTS):
        group //= 2
    assert group >= 1, "pages per sequence must be a multiple of the ring depth"
    grid_spec = pltpu.PrefetchScalarGridSpec(
        num_scalar_prefetch=1,
        grid=(n_seq,),
        in_specs=[pl.BlockSpec((1, HEADS, KV_LORA), lambda n, pt: (n, 0, 0)),
                  pl.BlockSpec((1, HEADS, QK_ROPE), lambda n, pt: (n, 0, 0)),
                  pl.BlockSpec((1, 1, KV_LORA), lambda n, pt: (n, 0, 0)),
                  pl.BlockSpec((1, 1, QK_ROPE), lambda n, pt: (n, 0, 0)),
                  pl.BlockSpec(memory_space=pl.ANY),
                  pl.BlockSpec(memory_space=pl.ANY)],
        out_specs=pl.BlockSpec((1, HEADS, KV_LORA), lambda n, pt: (n, 0, 0)),
        scratch_shapes=[pltpu.VMEM((DECODE_SLOTS, group, PAGE, KV_LORA), F32),
                        pltpu.VMEM((DECODE_SLOTS, QK_ROPE, group * PAGE), F32),
                        pltpu.SemaphoreType.DMA((2, DECODE_SLOTS))],
    )
    return pl.pallas_call(
        functools.partial(_decode_attn_kernel, layer=layer, n_seq=n_seq, n_pages=n_pages, group=group),
        grid_spec=grid_spec,
        out_shape=jax.ShapeDtypeStruct((n_seq, HEADS, KV_LORA), BF16),
        compiler_params=_params(1, 48),
        name="decode_attn",
    )(page_table.reshape(-1), q_lat, q_rope, c_new, kr_new, cache_c, cache_krt)


def _ssm_conv_kernel(x_ref, halo_ref, w_ref, b_ref, o_ref, xp_s):
    tm = x_ref.shape[0]
    halo = halo_ref[...]
    xp_s[0:SUBLANES, :] = jnp.where(pl.program_id(0) == 0, jnp.zeros_like(halo), halo)
    xp_s[SUBLANES:, :] = x_ref[...]
    w = w_ref[...]
    base = SUBLANES - (SSM_CONV - 1)
    y = xp_s[base:base + tm, :] * w[0:1]
    for k in range(1, SSM_CONV):
        y = y + xp_s[base + k:base + k + tm, :] * w[k:k + 1]
    o_ref[...] = _silu(y + b_ref[...])


def _ssm_conv(proj, w, b):
    t = proj.shape[0]
    tm, tn = min(t, 512), 512
    c0 = P_XBC // tn
    return pl.pallas_call(
        _ssm_conv_kernel,
        grid=(t // tm, CONV_DIM // tn),
        in_specs=[pl.BlockSpec((tm, tn), lambda i, j: (i, c0 + j)),
                  pl.BlockSpec((SUBLANES, tn), lambda i, j: (jnp.maximum(i * (tm // SUBLANES) - 1, 0), c0 + j)),
                  pl.BlockSpec((SSM_CONV, tn), lambda i, j: (0, j)),
                  pl.BlockSpec((1, tn), lambda i, j: (0, j))],
        out_specs=pl.BlockSpec((tm, tn), lambda i, j: (i, j)),
        out_shape=jax.ShapeDtypeStruct((t, CONV_DIM), F32),
        scratch_shapes=[pltpu.VMEM((tm + SUBLANES, tn), F32)],
        compiler_params=_params(2, 32),
        name="ssm_conv",
    )(proj, proj, w, b)


def _ssm_conv_step_kernel(st_ref, x_ref, w_ref, b_ref, o_ref):
    w = w_ref[...]
    y = st_ref[0] * w[0:1]
    for k in range(1, SSM_CONV - 1):
        y = y + st_ref[k] * w[k:k + 1]
    y = y + x_ref[...] * w[SSM_CONV - 1:SSM_CONV]
    o_ref[...] = _silu(y + b_ref[...])


def _ssm_conv_step(state_t, proj, w, b):
    n = proj.shape[0]
    tn = 512
    c0 = P_XBC // tn
    return pl.pallas_call(
        _ssm_conv_step_kernel,
        grid=(CONV_DIM // tn,),
        in_specs=[pl.BlockSpec((SSM_CONV - 1, n, tn), lambda j: (0, 0, j)),
                  pl.BlockSpec((n, tn), lambda j: (0, c0 + j)),
                  pl.BlockSpec((SSM_CONV, tn), lambda j: (0, j)),
                  pl.BlockSpec((1, tn), lambda j: (0, j))],
        out_specs=pl.BlockSpec((n, tn), lambda j: (0, j)),
        out_shape=jax.ShapeDtypeStruct((n, CONV_DIM), F32),
        compiler_params=_params(1, 32),
        name="ssm_conv_step",
    )(state_t, proj, w, b)


def _ssd_chunk_kernel(xbc_ref, sm_ref, dtb_ref, a_ref, dsk_ref, y_ref, hout_ref, st_s):
    c = pl.program_id(0)

    @pl.when(c == 0)
    def _():
        st_s[...] = jnp.zeros(st_s.shape, F32)

    ln = CHUNK
    dt = _softplus(sm_ref[:, LANES:2 * LANES] + dtb_ref[...])
    da = dt * a_ref[...]
    row = lax.broadcasted_iota(jnp.int32, (ln, ln), 0)
    col = lax.broadcasted_iota(jnp.int32, (ln, ln), 1)
    tri = row >= col
    cs = _dot_exact(tri.astype(F32), da)
    cs_t = cs.T
    ecs = jnp.exp(cs)
    cs_last = cs[ln - 1:ln, :]
    w_state = jnp.exp(cs_last - cs)
    e_last = jnp.exp(cs_last)
    lo_lane = lax.broadcasted_iota(jnp.int32, (ln, LANES), 1) < SSM_P
    lo_row = lax.broadcasted_iota(jnp.int32, (LANES, SSM_N), 0) < SSM_P

    def per_head(v, h0):
        return jnp.where(lo_lane, v[:, h0:h0 + 1], v[:, h0 + 1:h0 + 2])

    for g in range(SSM_GROUPS):
        bb = xbc_ref[:, D_INNER + g * SSM_N:D_INNER + (g + 1) * SSM_N].astype(BF16)
        cb = xbc_ref[:, D_INNER + GN + g * SSM_N:D_INNER + GN + (g + 1) * SSM_N].astype(BF16)
        cbt = _dot_nt(cb, bb)
        for kk in range(PAIRS_PER_GROUP):
            k = g * PAIRS_PER_GROUP + kk
            h0 = 2 * k
            sl = slice(k * LANES, (k + 1) * LANES)
            x2 = xbc_ref[:, sl]
            xd = x2 * per_head(dt, h0)
            xdb = xd.astype(BF16)

            def scores(h):
                d = cs[:, h:h + 1] - cs_t[h:h + 1, :]
                return (cbt * jnp.exp(jnp.where(tri, d, -jnp.inf))).astype(BF16)

            y_diag = jnp.where(lo_lane, _dot(scores(h0), xdb), _dot(scores(h0 + 1), xdb))
            h_prev = st_s[k]
            y_off = _dot_nt(cb, h_prev.astype(BF16)) * per_head(ecs, h0)
            y_ref[:, sl] = y_diag + y_off + dsk_ref[:, sl] * x2
            xw = xd * per_head(w_state, h0)
            st_new = _dot(xw.T.astype(BF16), bb)
            dec = jnp.where(lo_row, e_last[:, h0:h0 + 1], e_last[:, h0 + 1:h0 + 2])
            st_s[k] = h_prev * dec + st_new

    @pl.when(c == pl.num_programs(0) - 1)
    def _():
        hout_ref[...] = st_s[...]


def _ssd_chunked(xbc_act, proj, dt_bias_pad, a_pad, d_skip_cols):
    t = xbc_act.shape[0]
    return pl.pallas_call(
        _ssd_chunk_kernel,
        grid=(t // CHUNK,),
        in_specs=[pl.BlockSpec((CHUNK, CONV_DIM), lambda c: (c, 0)),
                  pl.BlockSpec((CHUNK, SMALL_W), lambda c: (c, P_SMALL // SMALL_W)),
                  pl.BlockSpec((1, LANES), lambda c: (0, 0)),
                  pl.BlockSpec((1, LANES), lambda c: (0, 0)),
                  pl.BlockSpec((1, D_INNER), lambda c: (0, 0))],
        out_specs=[pl.BlockSpec((CHUNK, D_INNER), lambda c: (c, 0)),
                   pl.BlockSpec((PAIRS, LANES, SSM_N), lambda c: (0, 0, 0))],
        out_shape=[jax.ShapeDtypeStruct((t, D_INNER), F32),
                   jax.ShapeDtypeStruct((PAIRS, LANES, SSM_N), F32)],
        scratch_shapes=[pltpu.VMEM((PAIRS, LANES, SSM_N), F32)],
        compiler_params=_params(1, 32),
        name="ssd_chunked",
    )(xbc_act, proj, dt_bias_pad, a_pad, d_skip_cols)


def _dt_prep_kernel(sm_ref, dtb_ref, a_ref, dt_ref, decay_ref):
    dt = _softplus(sm_ref[:, LANES:2 * LANES] + dtb_ref[...])
    dt_ref[...] = dt
    decay_ref[...] = jnp.exp(dt * a_ref[...])


def _dt_prep(proj, dt_bias_pad, a_pad):
    n = proj.shape[0]
    return pl.pallas_call(
        _dt_prep_kernel,
        grid=(1,),
        in_specs=[pl.BlockSpec((n, SMALL_W), lambda i: (0, P_SMALL // SMALL_W)),
                  pl.BlockSpec((1, LANES), lambda i: (0, 0)),
                  pl.BlockSpec((1, LANES), lambda i: (0, 0))],
        out_specs=[pl.BlockSpec((n, LANES), lambda i: (0, 0)), pl.BlockSpec((n, LANES), lambda i: (0, 0))],
        out_shape=[jax.ShapeDtypeStruct((n, LANES), F32), jax.ShapeDtypeStruct((n, LANES), F32)],
        compiler_params=_params(1, 32),
        name="dt_prep",
    )(proj, dt_bias_pad, a_pad)


def _ssd_step_kernel(decay_ref, st_ref, x_ref, b_ref, c_ref, dt_ref, dsk_ref, y_ref, so_ref):
    k = pl.program_id(0)
    n = x_ref.shape[0]
    head = lax.broadcasted_iota(jnp.int32, (LANES, LANES), 0)
    lane = lax.broadcasted_iota(jnp.int32, (LANES, LANES), 1)
    pick = (head == 2 * k + (lane >= SSM_P).astype(jnp.int32)).astype(F32)
    dt2 = _dot_exact(dt_ref[...], pick)
    x2 = x_ref[...]
    xd_t = (x2 * dt2).T
    c_t = c_ref[...].T.astype(BF16)
    ycols = jnp.zeros((LANES, n), F32)
    lane_n = lax.broadcasted_iota(jnp.int32, (LANES, n), 1)
    for t in range(n):
        upd = xd_t[:, t:t + 1] * b_ref[t:t + 1, :]
        new = []
        for hh in range(2):
            decay = decay_ref[t * SSM_HEADS + 2 * k + hh]
            new.append(st_ref[t, hh] * decay + upd[hh * SSM_P:(hh + 1) * SSM_P, :])
            so_ref[t, hh] = new[hh]
        y_all = _dot(jnp.concatenate(new, axis=0).astype(BF16), c_t)
        ycols = jnp.where(lane_n == t, y_all, ycols)
    y_ref[...] = ycols.T + dsk_ref[...] * x2


def _ssd_step(state, xbc_act, dt, decay, d_skip_cols):
    n = xbc_act.shape[0]
    assert n == LANES, "the per-sample SSD step transposes one (128, 128) tile of tokens"
    b0 = D_INNER // SSM_N
    c0 = (D_INNER + GN) // SSM_N
    st_spec = pl.BlockSpec((n, 2, SSM_P, SSM_N), lambda k: (0, k, 0, 0))
    return pl.pallas_call(
        _ssd_step_kernel,
        grid=(PAIRS,),
        in_specs=[pl.BlockSpec(memory_space=pltpu.SMEM),
                  st_spec,
                  pl.BlockSpec((n, LANES), lambda k: (0, k)),
                  pl.BlockSpec((n, SSM_N), lambda k: (0, b0 + k // PAIRS_PER_GROUP)),
                  pl.BlockSpec((n, SSM_N), lambda k: (0, c0 + k // PAIRS_PER_GROUP)),
                  pl.BlockSpec((n, LANES), lambda k: (0, 0)),
                  pl.BlockSpec((1, LANES), lambda k: (0, k))],
        out_specs=[pl.BlockSpec((n, LANES), lambda k: (0, k)), st_spec],
        out_shape=[jax.ShapeDtypeStruct((n, D_INNER), F32),
                   jax.ShapeDtypeStruct(state.shape, F32)],
        compiler_params=_params(1, 48),
        name="ssd_step",
    )(decay[:, :SSM_HEADS].reshape(-1), state, xbc_act, xbc_act, xbc_act, dt, d_skip_cols)


def _gate_norm_kernel(y_ref, z_ref, w_ref, o_ref):
    gw = D_INNER // SSM_GROUPS
    for g in range(SSM_GROUPS):
        sl = slice(g * gw, (g + 1) * gw)
        v = y_ref[:, sl] * _silu(z_ref[:, sl])
        o_ref[:, sl] = _rms(v, w_ref[:, sl]).astype(BF16)


def _gate_norm(y, proj, w):
    t = y.shape[0]
    tm = min(t, 512)
    return pl.pallas_call(
        _gate_norm_kernel,
        grid=(t // tm,),
        in_specs=[pl.BlockSpec((tm, D_INNER), lambda i: (i, 0)),
                  pl.BlockSpec((tm, D_INNER), lambda i: (i, P_Z // D_INNER)),
                  pl.BlockSpec((1, D_INNER), lambda i: (0, 0))],
        out_specs=pl.BlockSpec((tm, D_INNER), lambda i: (i, 0)),
        out_shape=jax.ShapeDtypeStruct((t, D_INNER), BF16),
        compiler_params=_params(1, 48),
        name="gate_norm",
    )(y, proj, w)


def _merge_kernel(oa_ref, os_ref, wa_ref, ws_ref, ga_ref, gs_ref, o_ref):
    a = _dot(oa_ref[...], wa_ref[...])
    s = _dot(os_ref[...], ws_ref[...])
    o_ref[...] = (jax.nn.sigmoid(ga_ref[...]) * a + jax.nn.sigmoid(gs_ref[...]) * s).astype(BF16)


def _merge(o_attn, o_ssm, w_attn, w_ssm, proj):
    t = o_attn.shape[0]
    tm, tn = min(t, 512), 512
    return pl.pallas_call(
        _merge_kernel,
        grid=(t // tm, D_MODEL // tn),
        in_specs=[pl.BlockSpec((tm, HEADS * V_DIM), lambda i, j: (i, 0)),
                  pl.BlockSpec((tm, D_INNER), lambda i, j: (i, 0)),
                  pl.BlockSpec((HEADS * V_DIM, tn), lambda i, j: (0, j)),
                  pl.BlockSpec((D_INNER, tn), lambda i, j: (0, j)),
                  pl.BlockSpec((tm, tn), lambda i, j: (i, P_GA // tn + j)),
                  pl.BlockSpec((tm, tn), lambda i, j: (i, P_GS // tn + j))],
        out_specs=pl.BlockSpec((tm, tn), lambda i, j: (i, j)),
        out_shape=jax.ShapeDtypeStruct((t, D_MODEL), BF16),
        compiler_params=_params(2, 48),
        name="merge",
    )(o_attn, o_ssm, w_attn, w_ssm, proj, proj)


def _out_proj_kernel(m_ref, w_ref, h_ref, npost_ref, npre_ref, h1_ref, hn_ref):
    h1 = h_ref[...] + _rms(_dot(m_ref[...], w_ref[...]), npost_ref[...])
    h1_ref[...] = h1
    hn_ref[...] = _rms(h1, npre_ref[...]).astype(BF16)


def _out_proj(merged, w_out, h, n_post, n_pre):
    t = h.shape[0]
    tm = min(t, 256)
    row = lambda i: (i, 0)
    fixed = lambda i: (0, 0)
    return pl.pallas_call(
        _out_proj_kernel,
        grid=(t // tm,),
        in_specs=[pl.BlockSpec((tm, D_MODEL), row), pl.BlockSpec((D_MODEL, D_MODEL), fixed),
                  pl.BlockSpec((tm, D_MODEL), row), pl.BlockSpec((1, D_MODEL), fixed),
                  pl.BlockSpec((1, D_MODEL), fixed)],
        out_specs=[pl.BlockSpec((tm, D_MODEL), row), pl.BlockSpec((tm, D_MODEL), row)],
        out_shape=[jax.ShapeDtypeStruct((t, D_MODEL), F32), jax.ShapeDtypeStruct((t, D_MODEL), BF16)],
        compiler_params=_params(1, 48),
        name="out_proj",
    )(merged, w_out, h, n_post, n_pre)


FFN_HALO = BF16_SUBLANES
FFN_ROW_CHUNKS = 2


def _ffn_kernel(x_ref, halo_ref, wg_ref, wv_ref, cwg_ref, cwv_ref, cbg_ref, cbv_ref, wd_ref, h_ref, nw_ref,
                o_ref, tg_ref, tv_ref, xe_s, u_s, acc_s):
    i, j = pl.program_id(0), pl.program_id(1)
    tm, tf = x_ref.shape[0], wg_ref.shape[1]

    @pl.when(j == 0)
    def _():
        halo = halo_ref[...]
        xe_s[0:FFN_HALO, :] = jnp.where(i == 0, jnp.zeros_like(halo), halo)
        xe_s[FFN_HALO:, :] = x_ref[...]
        acc_s[...] = jnp.zeros(acc_s.shape, F32)

    base = FFN_HALO - (FFN_CONV - 1)
    rc = tm // FFN_ROW_CHUNKS
    for c in range(FFN_ROW_CHUNKS):
        r0 = 0 if c == 0 else FFN_HALO + c * rc
        r1 = FFN_HALO + (c + 1) * rc
        xe = xe_s[r0:r1, :]
        u_s[r0:r1, 0:tf] = _dot(xe, wg_ref[...])
        u_s[r0:r1, tf:] = _dot(xe, wv_ref[...])
        t0 = base + c * rc

        def conv(lo, w_ref, b_ref):
            w = w_ref[...]
            y = u_s[t0:t0 + rc, lo:lo + tf] * w[0:1]
            for k in range(1, FFN_CONV):
                y = y + u_s[t0 + k:t0 + k + rc, lo:lo + tf] * w[k:k + 1]
            return y + b_ref[...]

        f = (_gelu_tanh(conv(0, cwg_ref, cbg_ref)) * conv(tf, cwv_ref, cbv_ref)).astype(BF16)
        acc_s[c * rc:(c + 1) * rc, :] += _dot(f, wd_ref[...])

    tail = FFN_HALO + tm - SUBLANES
    tg_ref[...] = u_s[tail:tail + SUBLANES, 0:tf]
    tv_ref[...] = u_s[tail:tail + SUBLANES, tf:]

    @pl.when(j == pl.num_programs(1) - 1)
    def _():
        o_ref[...] = h_ref[...] + _rms(acc_s[...], nw_ref[...])


def _ffn(hn, h, w_up, conv_w, conv_b, w_down, nw):
    t = h.shape[0]
    tm, tf = min(t, 512), 512
    nj = D_FF // tf
    gate = lambda i, j: (0, j)
    value = lambda i, j: (0, nj + j)
    row = lambda i, j: (i, 0)
    return pl.pallas_call(
        _ffn_kernel,
        grid=(t // tm, nj),
        in_specs=[pl.BlockSpec((tm, D_MODEL), row),
                  pl.BlockSpec((FFN_HALO, D_MODEL), lambda i, j: (jnp.maximum(i * (tm // FFN_HALO) - 1, 0), 0)),
                  pl.BlockSpec((D_MODEL, tf), gate), pl.BlockSpec((D_MODEL, tf), value),
                  pl.BlockSpec((FFN_CONV, tf), gate), pl.BlockSpec((FFN_CONV, tf), value),
                  pl.BlockSpec((1, tf), gate), pl.BlockSpec((1, tf), value),
                  pl.BlockSpec((tf, D_MODEL), lambda i, j: (j, 0)),
                  pl.BlockSpec((tm, D_MODEL), row),
                  pl.BlockSpec((1, D_MODEL), lambda i, j: (0, 0))],
        out_specs=[pl.BlockSpec((tm, D_MODEL), row),
                   pl.BlockSpec((SUBLANES, tf), lambda i, j: (i, j)),
                   pl.BlockSpec((SUBLANES, tf), lambda i, j: (i, j))],
        out_shape=[jax.ShapeDtypeStruct((t, D_MODEL), F32),
                   jax.ShapeDtypeStruct((t // tm * SUBLANES, D_FF), F32),
                   jax.ShapeDtypeStruct((t // tm * SUBLANES, D_FF), F32)],
        scratch_shapes=[pltpu.VMEM((tm + FFN_HALO, D_MODEL), BF16),
                        pltpu.VMEM((tm + FFN_HALO, 2 * tf), F32),
                        pltpu.VMEM((tm, D_MODEL), F32)],
        compiler_params=_params(2, 56),
        name="ffn",
    )(hn, hn, w_up, w_up, conv_w, conv_w, conv_b, conv_b, w_down, h, nw)


def _ffn_step_kernel(x_ref, wg_ref, wv_ref, sg_ref, sv_ref, cwg_ref, cwv_ref, cbg_ref, cbv_ref, wd_ref, h_ref,
                     nw_ref, o_ref, ug_ref, uv_ref, acc_s):
    j = pl.program_id(0)
    x = x_ref[...]

    def conv(u, s_ref, w_ref, b_ref):
        w = w_ref[...]
        y = s_ref[0] * w[0:1]
        for k in range(1, FFN_CONV - 1):
            y = y + s_ref[k] * w[k:k + 1]
        return y + u * w[FFN_CONV - 1:FFN_CONV] + b_ref[...]

    ug = _dot(x, wg_ref[...])
    uv = _dot(x, wv_ref[...])
    ug_ref[...] = ug
    uv_ref[...] = uv
    f = (_gelu_tanh(conv(ug, sg_ref, cwg_ref, cbg_ref)) * conv(uv, sv_ref, cwv_ref, cbv_ref)).astype(BF16)
    d = _dot(f, wd_ref[...])

    @pl.when(j == 0)
    def _():
        acc_s[...] = d

    @pl.when(j > 0)
    def _():
        acc_s[...] += d

    @pl.when(j == pl.num_programs(0) - 1)
    def _():
        o_ref[...] = h_ref[...] + _rms(acc_s[...], nw_ref[...])


def _ffn_step(hn, h, state_t, w_up, conv_w, conv_b, w_down, nw):
    n = h.shape[0]
    tf = 512
    nj = D_FF // tf
    gate = lambda j: (0, j)
    value = lambda j: (0, nj + j)
    fixed = lambda j: (0, 0)
    return pl.pallas_call(
        _ffn_step_kernel,
        grid=(nj,),
        in_specs=[pl.BlockSpec((n, D_MODEL), fixed),
                  pl.BlockSpec((D_MODEL, tf), gate), pl.BlockSpec((D_MODEL, tf), value),
                  pl.BlockSpec((FFN_CONV - 1, n, tf), lambda j: (0, 0, j)),
                  pl.BlockSpec((FFN_CONV - 1, n, tf), lambda j: (0, 0, nj + j)),
                  pl.BlockSpec((FFN_CONV, tf), gate), pl.BlockSpec((FFN_CONV, tf), value),
                  pl.BlockSpec((1, tf), gate), pl.BlockSpec((1, tf), value),
                  pl.BlockSpec((tf, D_MODEL), lambda j: (j, 0)),
                  pl.BlockSpec((n, D_MODEL), fixed),
                  pl.BlockSpec((1, D_MODEL), fixed)],
        out_specs=[pl.BlockSpec((n, D_MODEL), fixed), pl.BlockSpec((n, tf), gate), pl.BlockSpec((n, tf), gate)],
        out_shape=[jax.ShapeDtypeStruct((n, D_MODEL), F32),
                   jax.ShapeDtypeStruct((n, D_FF), F32), jax.ShapeDtypeStruct((n, D_FF), F32)],
        scratch_shapes=[pltpu.VMEM((n, D_MODEL), F32)],
        compiler_params=_params(1, 32),
        name="ffn_step",
    )(hn, w_up, w_up, state_t, state_t, conv_w, conv_w, conv_b, conv_b, w_down, h, nw)


def _ple_kernel(h_ref, pe_ref, wg_ref, wp_ref, nw_ref, o_ref):
    h = h_ref[...]
    gate = jax.nn.sigmoid(_dot(h.astype(BF16), wg_ref[...]))
    o_ref[...] = h + _rms(gate * _dot(pe_ref[...].astype(BF16), wp_ref[...]), nw_ref[...])


def _ple(h, pe, w_gate, w_proj, nw):
    t = h.shape[0]
    tm = min(t, 256)
    row = lambda i: (i, 0)
    fixed = lambda i: (0, 0)
    return pl.pallas_call(
        _ple_kernel,
        grid=(t // tm,),
        in_specs=[pl.BlockSpec((tm, D_MODEL), row), pl.BlockSpec((tm, PLE_DIM), row),
                  pl.BlockSpec((D_MODEL, D_MODEL), fixed), pl.BlockSpec((PLE_DIM, D_MODEL), fixed),
                  pl.BlockSpec((1, D_MODEL), fixed)],
        out_specs=pl.BlockSpec((tm, D_MODEL), row),
        out_shape=jax.ShapeDtypeStruct((t, D_MODEL), F32),
        compiler_params=_params(1, 48),
        name="ple",
    )(h, pe, w_gate, w_proj, nw)


def _rot_half_cols(w):
    half = QK_ROPE // 2
    return jnp.concatenate([-w[..., half:], w[..., :half]], axis=-1)


def _prepare_weights(w):
    splits = np.cumsum([0, Q_LORA, KV_LORA, QK_ROPE, D_INNER, CONV_DIM, SSM_HEADS, D_MODEL, D_MODEL])
    q_c, kv_c, k_r, z, xbc, dt, g_a, g_s = [w['w_in'][:, a:b] for a, b in zip(splits[:-1], splits[1:])]
    pad = jnp.zeros((D_MODEL, SMALL_W - 2 * QK_ROPE - SSM_HEADS), F32)
    packed = jnp.concatenate([z, g_a, g_s, xbc, kv_c, k_r, _rot_half_cols(k_r), dt, pad, q_c], axis=1)
    wq = w['w_q_up'].reshape(Q_LORA, HEADS, QK_NOPE + QK_ROPE)
    wq_rope = wq[:, :, QK_NOPE:]
    wkv = w['w_kv_up'].reshape(KV_LORA, HEADS, QK_NOPE + V_DIM)
    lane_pad = LANES - SSM_HEADS
    return dict(
        w_in=packed.astype(BF16),
        wq_nope=wq[:, :, :QK_NOPE].reshape(Q_LORA, HEADS * QK_NOPE).astype(BF16),
        wq_rope=wq_rope.reshape(Q_LORA, HEADS * QK_ROPE).astype(BF16),
        wq_rope_rot=_rot_half_cols(wq_rope).reshape(Q_LORA, HEADS * QK_ROPE).astype(BF16),
        w_nope=wkv[:, :, :QK_NOPE].transpose(1, 2, 0).astype(BF16),
        w_v=wkv[:, :, QK_NOPE:].transpose(1, 0, 2).astype(BF16),
        w_branch_attn=w['w_branch_attn'].astype(BF16),
        w_branch_ssm=w['w_branch_ssm'].astype(BF16),
        w_out=w['w_out'].astype(BF16),
        w_ffn_up=w['w_ffn_up'].astype(BF16),
        w_ffn_down=w['w_ffn_down'].astype(BF16),
        w_ple_gate=w['w_ple_gate'].astype(BF16),
        w_ple_proj=w['w_ple_proj'].astype(BF16),
        dt_bias=jnp.pad(w['dt_bias'], (0, lane_pad))[None, :],
        a_neg=jnp.pad(-jnp.exp(w['a_log']), (0, lane_pad))[None, :],
        d_skip=jnp.repeat(w['d_skip'], SSM_P)[None, :],
    )


def _rope_tables(pos):
    half = QK_ROPE // 2
    inv = ROPE_THETA ** (-jnp.arange(half, dtype=F32) / half)
    ang = pos[:, None] * inv[None, :]
    cos, sin = jnp.cos(ang), jnp.sin(ang)
    cos64, sin64 = jnp.concatenate([cos, cos], axis=1), jnp.concatenate([sin, sin], axis=1)
    return cos64, sin64, jnp.concatenate([cos64, cos64], axis=1), jnp.concatenate([sin64, sin64], axis=1)


def _vec(v):
    return v[None, :]


def _mixer_front(h, pos, w, pw):
    proj = _in_proj(h, _vec(w['norm_pre_mix']), pw['w_in'])
    cos64, sin64, cos128, sin128 = _rope_tables(pos)
    c, c_bf, kr, kr_bf = _kv_prep(proj, cos64, sin64, _vec(w['norm_kv']))
    q_nope, q_rope = _q_prep(proj, _vec(w['norm_q']), pw['wq_nope'], pw['wq_rope'], pw['wq_rope_rot'],
                             cos128, sin128)
    return proj, c, c_bf, kr, kr_bf, q_nope, q_rope


def _mixer_back(h, proj, o_attn, y_ssm, w, pw):
    o_ssm = _gate_norm(y_ssm, proj, _vec(w['norm_ssm']))
    merged = _merge(o_attn, o_ssm, pw['w_branch_attn'], pw['w_branch_ssm'], proj)
    return _out_proj(merged, pw['w_out'], h, _vec(w['norm_post_mix']), _vec(w['norm_pre_ffn']))


def _prompt_layer(h, pe, w, pw):
    t = h.shape[0]
    pos = jnp.arange(t, dtype=F32)
    proj, c, c_bf, kr, kr_bf, q_nope, q_rope = _mixer_front(h, pos, w, pw)
    o_attn = _prompt_attn(q_nope, q_rope, pw['w_nope'], pw['w_v'], c_bf, kr_bf)
    xbc_act = _ssm_conv(proj, w['conv_ssm_w'], _vec(w['conv_ssm_b']))
    y_ssm, st = _ssd_chunked(xbc_act, proj, pw['dt_bias'], pw['a_neg'], pw['d_skip'])
    h1, hn1 = _mixer_back(h, proj, o_attn, y_ssm, w, pw)
    h2, tail_g, tail_v = _ffn(hn1, h1, pw['w_ffn_up'], w['conv_ffn_w'], _vec(w['conv_ffn_b']),
                              pw['w_ffn_down'], _vec(w['norm_post_ffn']))
    h3 = _ple(h2, pe, pw['w_ple_gate'], pw['w_ple_proj'], _vec(w['norm_ple']))
    ssm_conv_new = proj[t - (SSM_CONV - 1):, P_XBC:P_XBC + CONV_DIM]
    last = tail_g.shape[0] - (FFN_CONV - 1)
    ffn_conv_new = jnp.concatenate([tail_g[last:], tail_v[last:]], axis=1)
    return h3, (c, kr, st.reshape(SSM_HEADS, SSM_P, SSM_N), ssm_conv_new, ffn_conv_new)


def _sample_layer(h, pe, past, w, pw, cache_c, cache_kr, layer, page_table, ssm_state, ssm_conv_state,
                  ffn_conv_state):
    n = h.shape[0]
    pos = jnp.full((n,), past, F32)
    proj, c, c_bf, kr, kr_bf, q_nope, q_rope = _mixer_front(h, pos, w, pw)
    q_lat = _q_latent_sample(q_nope, pw['w_nope']).transpose(1, 0, 2)
    o_lat = _decode_attn(page_table, q_lat, q_rope.reshape(n, HEADS, QK_ROPE), c_bf[:, None, :],
                         kr_bf[:, None, :], cache_c, jnp.swapaxes(cache_kr, 2, 3), layer)
    o_attn = _o_attn_sample(o_lat.transpose(1, 0, 2), pw['w_v'])
    xbc_act = _ssm_conv_step(ssm_conv_state.transpose(1, 0, 2), proj, w['conv_ssm_w'], _vec(w['conv_ssm_b']))
    dt, decay = _dt_prep(proj, pw['dt_bias'], pw['a_neg'])
    y_ssm, st = _ssd_step(ssm_state, xbc_act, dt, decay, pw['d_skip'])
    h1, hn1 = _mixer_back(h, proj, o_attn, y_ssm, w, pw)
    h2, u_g, u_v = _ffn_step(hn1, h1, ffn_conv_state.transpose(1, 0, 2), pw['w_ffn_up'], w['conv_ffn_w'],
                             _vec(w['conv_ffn_b']), pw['w_ffn_down'], _vec(w['norm_post_ffn']))
    h3 = _ple(h2, pe, pw['w_ple_gate'], pw['w_ple_proj'], _vec(w['norm_ple']))
    xbc_raw = proj[:, P_XBC:P_XBC + CONV_DIM]
    ssm_conv_new = jnp.concatenate([ssm_conv_state[:, 1:], xbc_raw[:, None, :]], axis=1)
    u = jnp.concatenate([u_g, u_v], axis=1)
    ffn_conv_new = jnp.concatenate([ffn_conv_state[:, 1:], u[:, None, :]], axis=1)
    return h3, (c[:, None, :], kr[:, None, :], st, ssm_conv_new, ffn_conv_new)


def kernel(x_prompt, x_sample, cache_kv_latent, cache_k_rope, state_ssm, state_ssm_conv, state_ffn_conv, page_table, p_prompt, p_sample, norm_pre_mix, w_in, norm_q, w_q_up, norm_kv, w_kv_up, w_branch_attn, conv_ssm_w, conv_ssm_b, dt_bias, a_log, d_skip, norm_ssm, w_branch_ssm, w_out, norm_post_mix, norm_pre_ffn, w_ffn_up, conv_ffn_w, conv_ffn_b, w_ffn_down, norm_post_ffn, w_ple_gate, w_ple_proj, norm_ple):
    assert x_prompt.shape[0] == 1, "one prompt sequence"
    assert x_sample.shape[1] == 1, "one new token per sample sequence"
    depth = w_in.shape[0]
    past = page_table.shape[1] * PAGE
    weights = dict(norm_pre_mix=norm_pre_mix, w_in=w_in, norm_q=norm_q, w_q_up=w_q_up, norm_kv=norm_kv,
                   w_kv_up=w_kv_up, w_branch_attn=w_branch_attn, conv_ssm_w=conv_ssm_w, conv_ssm_b=conv_ssm_b,
                   dt_bias=dt_bias, a_log=a_log, d_skip=d_skip, norm_ssm=norm_ssm, w_branch_ssm=w_branch_ssm,
                   w_out=w_out, norm_post_mix=norm_post_mix, norm_pre_ffn=norm_pre_ffn, w_ffn_up=w_ffn_up,
                   conv_ffn_w=conv_ffn_w, conv_ffn_b=conv_ffn_b, w_ffn_down=w_ffn_down,
                   norm_post_ffn=norm_post_ffn, w_ple_gate=w_ple_gate, w_ple_proj=w_ple_proj, norm_ple=norm_ple)
    hp, hs = x_prompt[0], x_sample[:, 0]
    outs_p, outs_s = [], []
    for d in range(depth):
        w = {k: v[d] for k, v in weights.items()}
        pw = _prepare_weights(w)
        hp, st_p = _prompt_layer(hp, p_prompt[d, 0], w, pw)
        hs, st_s = _sample_layer(hs, p_sample[d, :, 0], past, w, pw, cache_kv_latent, cache_k_rope, d,
                                 page_table, state_ssm[d], state_ssm_conv[d], state_ffn_conv[d])
        outs_p.append(st_p)
        outs_s.append(st_s)
    stack_p = [jnp.stack([o[k] for o in outs_p])[:, None] for k in range(5)]
    stack_s = [jnp.stack([o[k] for o in outs_s]) for k in range(5)]
    return (hp[None], hs[:, None], *stack_p, *stack_s)
```

```python
import functools

import jax
import jax.numpy as jnp
import numpy as np
from jax import lax
from jax.experimental import pallas as pl
from jax.experimental.pallas import tpu as pltpu

F32, BF16 = jnp.float32, jnp.bfloat16
HIGHEST = lax.Precision.HIGHEST

LANES = 128
SUBLANES = 8
BF16_SUBLANES = 16
V7X_VMEM_BYTES = 64 * 1024 * 1024

D_MODEL = 2048
PAGE = 128
HEADS = 16
QK_NOPE = 128
QK_ROPE = 64
V_DIM = 128
Q_LORA = 768
KV_LORA = 512
ROPE_THETA = 10000.0
ATTN_SCALE = (QK_NOPE + QK_ROPE) ** -0.5
SOFTMAX_LOG2_SCALE = ATTN_SCALE * float(np.log2(np.e))
SSM_HEADS = 32
SSM_P = 64
D_INNER = SSM_HEADS * SSM_P
SSM_GROUPS = 4
SSM_N = 128
SSM_CONV = 4
GN = SSM_GROUPS * SSM_N
CONV_DIM = D_INNER + 2 * GN
CHUNK = 128
PAIRS = SSM_HEADS // 2
PAIRS_PER_GROUP = PAIRS // SSM_GROUPS
D_FF = 4 * D_MODEL
FFN_CONV = 3
PLE_DIM = 256
EPS = 1e-6

P_Z = 0
P_GA = P_Z + D_INNER
P_GS = P_GA + D_MODEL
P_XBC = P_GS + D_MODEL
P_KV = P_XBC + CONV_DIM
P_SMALL = P_KV + KV_LORA
P_Q = P_SMALL + 2 * LANES
P_COLS = P_Q + Q_LORA
SMALL_W = 2 * LANES


def _params(n_axes, vmem_mib):
    return pltpu.CompilerParams(dimension_semantics=("arbitrary",) * n_axes,
                                vmem_limit_bytes=vmem_mib * 1024 * 1024)


def _rms(x, w):
    return x * lax.rsqrt(jnp.mean(x * x, axis=-1, keepdims=True) + EPS) * w


def _dot(a, b):
    return jnp.dot(a, b, preferred_element_type=F32)


def _dot_nt(a, b):
    return lax.dot_general(a, b, (((1,), (1,)), ((), ())), preferred_element_type=F32)


def _dot_exact(a, b):
    return jnp.dot(a, b, precision=HIGHEST, preferred_element_type=F32)


def _rep(x, n):
    return x if n == 1 else jnp.concatenate([x] * n, axis=1)


def _silu(x):
    return x * jax.nn.sigmoid(x)


def _softplus(x):
    return jnp.maximum(x, 0.0) + jnp.log1p(jnp.exp(-jnp.abs(x)))


def _gelu_tanh(x):
    return x * (0.5 * (1.0 + jnp.tanh(np.sqrt(2.0 / np.pi).astype(np.float32) * (x + 0.044715 * (x * x * x)))))


def _in_proj_kernel(x_ref, nw_ref, w_ref, o_ref, xn_ref):
    @pl.when(pl.program_id(1) == 0)
    def _():
        xn_ref[...] = _rms(x_ref[...], nw_ref[...]).astype(BF16)

    o_ref[...] = _dot_nt(xn_ref[...], w_ref[...])


def _in_proj(x, nw, w_packed_t):
    t = x.shape[0]
    tm, tn = min(t, 1024), 1536
    return pl.pallas_call(
        _in_proj_kernel,
        grid=(t // tm, P_COLS // tn),
        in_specs=[pl.BlockSpec((tm, D_MODEL), lambda i, j: (i, 0)),
                  pl.BlockSpec((1, D_MODEL), lambda i, j: (0, 0)),
                  pl.BlockSpec((tn, D_MODEL), lambda i, j: (j, 0))],
        out_specs=pl.BlockSpec((tm, tn), lambda i, j: (i, j)),
        out_shape=jax.ShapeDtypeStruct((t, P_COLS), F32),
        scratch_shapes=[pltpu.VMEM((tm, D_MODEL), BF16)],
        compiler_params=_params(2, 56),
        name="in_proj",
    )(x, nw, w_packed_t)


def _kv_prep_kernel(kv_ref, sm_ref, cos_ref, sin_ref, nw_ref, c_ref, cbf_ref, kr_ref, krbf_ref):
    c = _rms(kv_ref[...], nw_ref[...])
    c_ref[...] = c
    cbf_ref[...] = c.astype(BF16)
    sm = sm_ref[...]
    kr = sm[:, 0:QK_ROPE] * cos_ref[...] + sm[:, QK_ROPE:2 * QK_ROPE] * sin_ref[...]
    kr_ref[...] = kr
    krbf_ref[...] = kr.astype(BF16)


def _kv_prep(proj, cos64, sin64, nw):
    t = proj.shape[0]
    tm = min(t, 512)
    return pl.pallas_call(
        _kv_prep_kernel,
        grid=(t // tm,),
        in_specs=[pl.BlockSpec((tm, KV_LORA), lambda i: (i, P_KV // KV_LORA)),
                  pl.BlockSpec((tm, SMALL_W), lambda i: (i, P_SMALL // SMALL_W)),
                  pl.BlockSpec((tm, QK_ROPE), lambda i: (i, 0)),
                  pl.BlockSpec((tm, QK_ROPE), lambda i: (i, 0)),
                  pl.BlockSpec((1, KV_LORA), lambda i: (0, 0))],
        out_specs=[pl.BlockSpec((tm, KV_LORA), lambda i: (i, 0)),
                   pl.BlockSpec((tm, KV_LORA), lambda i: (i, 0)),
                   pl.BlockSpec((tm, QK_ROPE), lambda i: (i, 0)),
                   pl.BlockSpec((tm, QK_ROPE), lambda i: (i, 0))],
        out_shape=[jax.ShapeDtypeStruct((t, KV_LORA), F32), jax.ShapeDtypeStruct((t, KV_LORA), BF16),
                   jax.ShapeDtypeStruct((t, QK_ROPE), F32), jax.ShapeDtypeStruct((t, QK_ROPE), BF16)],
        compiler_params=_params(1, 32),
        name="kv_prep",
    )(proj, proj, cos64, sin64, nw)


def _q_prep_kernel(q_ref, nw_ref, wn_ref, wr_ref, wrr_ref, cos_ref, sin_ref, qn_ref, qr_ref):
    qn = _rms(q_ref[...], nw_ref[...]).astype(BF16)
    qn_ref[...] = _dot(qn, wn_ref[...]).astype(BF16)
    r = _dot(qn, wr_ref[...])
    rr = _dot(qn, wrr_ref[...])
    cos, sin = cos_ref[...], sin_ref[...]
    for k in range(HEADS * QK_ROPE // LANES):
        sl = slice(k * LANES, (k + 1) * LANES)
        qr_ref[:, sl] = (r[:, sl] * cos + rr[:, sl] * sin).astype(BF16)


def _q_prep(proj, nw, wq_nope, wq_rope, wq_rope_rot, cos128, sin128):
    t = proj.shape[0]
    tm = min(t, 512)
    n_nope, n_rope = HEADS * QK_NOPE, HEADS * QK_ROPE
    return pl.pallas_call(
        _q_prep_kernel,
        grid=(t // tm,),
        in_specs=[pl.BlockSpec((tm, Q_LORA), lambda i: (i, P_Q // Q_LORA)),
                  pl.BlockSpec((1, Q_LORA), lambda i: (0, 0)),
                  pl.BlockSpec((Q_LORA, n_nope), lambda i: (0, 0)),
                  pl.BlockSpec((Q_LORA, n_rope), lambda i: (0, 0)),
                  pl.BlockSpec((Q_LORA, n_rope), lambda i: (0, 0)),
                  pl.BlockSpec((tm, LANES), lambda i: (i, 0)),
                  pl.BlockSpec((tm, LANES), lambda i: (i, 0))],
        out_specs=[pl.BlockSpec((tm, n_nope), lambda i: (i, 0)),
                   pl.BlockSpec((tm, n_rope), lambda i: (i, 0))],
        out_shape=[jax.ShapeDtypeStruct((t, n_nope), BF16), jax.ShapeDtypeStruct((t, n_rope), BF16)],
        compiler_params=_params(1, 48),
        name="q_prep",
    )(proj, nw, wq_nope, wq_rope, wq_rope_rot, cos128, sin128)


def _prompt_attn_kernel(qi_ref, kj_ref, qn_ref, qr_ref, wn_ref, wv_ref, c_ref, kr_ref, o_ref,
                        qlat_s, qrope_s, m_s, l_s, acc_s, *, tq):
    step = pl.program_id(0)
    i = qi_ref[step]
    j = kj_ref[step]
    j_last = i // 4
    r = i % 4

    @pl.when(j == 0)
    def _init():
        for h in range(HEADS):
            rs = slice(h * tq, (h + 1) * tq)
            qlat_s[rs, :] = _dot(qn_ref[:, h * QK_NOPE:(h + 1) * QK_NOPE], wn_ref[h]).astype(BF16)
            qrope_s[rs, :] = qr_ref[:, h * QK_ROPE:(h + 1) * QK_ROPE]
        m_s[...] = jnp.full(m_s.shape, -jnp.inf, F32)
        l_s[...] = jnp.zeros(l_s.shape, F32)
        acc_s[...] = jnp.zeros(acc_s.shape, F32)

    def scores(k0, nk):
        return (_dot_nt(qlat_s[...], c_ref[k0:k0 + nk, :])
                + _dot_nt(qrope_s[...], kr_ref[k0:k0 + nk, :])) * SOFTMAX_LOG2_SCALE

    def update(s, k0, nk, limit=None):
        if limit is not None:
            visible = (lax.broadcasted_iota(jnp.int32, (1, tq, nk), 2)
                       <= lax.broadcasted_iota(jnp.int32, (1, tq, nk), 1) + limit)
            s = jnp.where(visible, s.reshape(HEADS, tq, nk), -jnp.inf).reshape(HEADS * tq, nk)
        m_prev = m_s[...]
        m_new = jnp.maximum(m_prev, jnp.max(s, axis=-1, keepdims=True))
        alpha = jnp.exp2(m_prev - m_new)
        p = jnp.exp2(s - _rep(m_new, nk // LANES))
        l_s[...] = alpha * l_s[...] + jnp.sum(p, axis=-1, keepdims=True)
        acc_s[...] = acc_s[...] * _rep(alpha, KV_LORA // LANES) + _dot(p.astype(BF16), c_ref[k0:k0 + nk, :])
        m_s[...] = m_new

    @pl.when(j < j_last)
    def _below_diagonal():
        s_a = scores(0, 2 * tq)
        s_b = scores(2 * tq, 2 * tq)
        update(s_a, 0, 2 * tq)
        update(s_b, 2 * tq, 2 * tq)

    def quarter_pair(first, limit_of_second):
        s_a = scores(first * tq, tq)
        s_b = scores((first + 1) * tq, tq)
        update(s_a, first * tq, tq)
        update(s_b, (first + 1) * tq, tq, limit=limit_of_second)

    diagonal = j == j_last

    @pl.when(jnp.logical_and(diagonal, r == 0))
    def _diagonal_quarter_0():
        update(scores(0, tq), 0, tq, limit=0)

    @pl.when(jnp.logical_and(diagonal, r >= 1))
    def _quarters_0_1():
        quarter_pair(0, (r - 1) * tq)

    @pl.when(jnp.logical_and(diagonal, r == 2))
    def _diagonal_quarter_2():
        update(scores(2 * tq, tq), 2 * tq, tq, limit=0)

    @pl.when(jnp.logical_and(diagonal, r == 3))
    def _quarters_2_3():
        quarter_pair(2, 0)

    @pl.when(j == j_last)
    def _finish():
        for h in range(HEADS):
            rs = slice(h * tq, (h + 1) * tq)
            o_lat = (acc_s[rs, :] / _rep(l_s[rs, :], KV_LORA // LANES)).astype(BF16)
            o_ref[:, h * V_DIM:(h + 1) * V_DIM] = _dot(o_lat, wv_ref[h]).astype(BF16)


def _prompt_attn(q_nope, q_rope, w_nope, w_v, c_bf, kr_bf):
    t = q_nope.shape[0]
    tq = 256
    tk = 4 * tq
    assert t % tk == 0, "prompt length must be a multiple of the key block"
    pairs = [(i, j) for i in range(t // tq) for j in range(i // 4 + 1)]
    qi = jnp.asarray([p[0] for p in pairs], jnp.int32)
    kj = jnp.asarray([p[1] for p in pairs], jnp.int32)
    rows = HEADS * tq
    grid_spec = pltpu.PrefetchScalarGridSpec(
        num_scalar_prefetch=2,
        grid=(len(pairs),),
        in_specs=[pl.BlockSpec((tq, HEADS * QK_NOPE), lambda s, qi, kj: (qi[s], 0)),
                  pl.BlockSpec((tq, HEADS * QK_ROPE), lambda s, qi, kj: (qi[s], 0)),
                  pl.BlockSpec((HEADS, QK_NOPE, KV_LORA), lambda s, qi, kj: (0, 0, 0)),
                  pl.BlockSpec((HEADS, KV_LORA, V_DIM), lambda s, qi, kj: (0, 0, 0)),
                  pl.BlockSpec((tk, KV_LORA), lambda s, qi, kj: (kj[s], 0)),
                  pl.BlockSpec((tk, QK_ROPE), lambda s, qi, kj: (kj[s], 0))],
        out_specs=pl.BlockSpec((tq, HEADS * V_DIM), lambda s, qi, kj: (qi[s], 0)),
        scratch_shapes=[pltpu.VMEM((rows, KV_LORA), BF16), pltpu.VMEM((rows, QK_ROPE), BF16),
                        pltpu.VMEM((rows, LANES), F32), pltpu.VMEM((rows, LANES), F32),
                        pltpu.VMEM((rows, KV_LORA), F32)],
    )
    return pl.pallas_call(
        functools.partial(_prompt_attn_kernel, tq=tq),
        grid_spec=grid_spec,
        out_shape=jax.ShapeDtypeStruct((t, HEADS * V_DIM), BF16),
        compiler_params=_params(1, 56),
        name="prompt_attn",
    )(qi, kj, q_nope, q_rope, w_nope, w_v, c_bf, kr_bf)


def _head_proj_in_kernel(x_ref, w_ref, o_ref):
    o_ref[0] = _dot(x_ref[...], w_ref[0]).astype(BF16)


def _q_latent_sample(q_nope, w_nope):
    n = q_nope.shape[0]
    return pl.pallas_call(
        _head_proj_in_kernel,
        grid=(HEADS,),
        in_specs=[pl.BlockSpec((n, QK_NOPE), lambda h: (0, h)),
                  pl.BlockSpec((1, QK_NOPE, KV_LORA), lambda h: (h, 0, 0))],
        out_specs=pl.BlockSpec((1, n, KV_LORA), lambda h: (h, 0, 0)),
        out_shape=jax.ShapeDtypeStruct((HEADS, n, KV_LORA), BF16),
        compiler_params=_params(1, 32),
        name="q_latent_sample",
    )(q_nope, w_nope)


def _head_proj_out_kernel(x_ref, w_ref, o_ref):
    o_ref[...] = _dot(x_ref[0], w_ref[0]).astype(BF16)


def _o_attn_sample(o_lat, w_v):
    n = o_lat.shape[1]
    return pl.pallas_call(
        _head_proj_out_kernel,
        grid=(HEADS,),
        in_specs=[pl.BlockSpec((1, n, KV_LORA), lambda h: (h, 0, 0)),
                  pl.BlockSpec((1, KV_LORA, V_DIM), lambda h: (h, 0, 0))],
        out_specs=pl.BlockSpec((n, V_DIM), lambda h: (0, h)),
        out_shape=jax.ShapeDtypeStruct((n, HEADS * V_DIM), BF16),
        compiler_params=_params(1, 32),
        name="o_attn_sample",
    )(o_lat, w_v)


DECODE_SLOTS = 4


def _decode_attn_kernel(pt_ref, qlat_ref, qr_ref, cnew_ref, krnew_ref, cache_c, cache_krt, o_ref,
                        cbuf, krbuf, sems, *, layer, n_seq, n_pages, group):
    n = pl.program_id(0)
    n_chunks = n_pages // group
    keys = group * PAGE
    ahead = DECODE_SLOTS - 1

    def page_copies(seq, chunk, g):
        slot = chunk % DECODE_SLOTS
        page = pt_ref[seq * n_pages + chunk * group + g]
        return (pltpu.make_async_copy(cache_c.at[layer, page], cbuf.at[slot, g], sems.at[0, slot]),
                pltpu.make_async_copy(cache_krt.at[layer, page], krbuf.at[slot, :, pl.ds(g * PAGE, PAGE)],
                                      sems.at[1, slot]))

    def start(seq, chunk):
        for g in range(group):
            for cp in page_copies(seq, chunk, g):
                cp.start()

    def wait(seq, chunk):
        for g in range(group):
            for cp in page_copies(seq, chunk, g):
                cp.wait()

    @pl.when(n == 0)
    def _():
        for ch in range(ahead):
            start(n, ch)

    q = qlat_ref[0]
    qr = qr_ref[0]
    m = jnp.full((HEADS, 1), -jnp.inf, F32)
    l = jnp.zeros((HEADS, 1), F32)
    acc = jnp.zeros((HEADS, KV_LORA), F32)
    for ch in range(n_chunks):
        nxt = ch + ahead
        if nxt < n_chunks:
            start(n, nxt)
        else:
            @pl.when(n + 1 < n_seq)
            def _():
                start(n + 1, nxt - n_chunks)
        wait(n, ch)
        slot = ch % DECODE_SLOTS
        cb = cbuf[slot].reshape(keys, KV_LORA).astype(BF16)
        krt = krbuf[slot].astype(BF16)
        s = (_dot_nt(q, cb) + _dot(qr, krt)) * SOFTMAX_LOG2_SCALE
        m_new = jnp.maximum(m, jnp.max(s, axis=-1, keepdims=True))
        alpha = jnp.exp2(m - m_new)
        p = jnp.exp2(s - m_new)
        l = alpha * l + jnp.sum(p, axis=-1, keepdims=True)
        acc = alpha * acc + _dot(p.astype(BF16), cb)
        m = m_new
    c_new = cnew_ref[0].astype(F32)
    s_new = (jnp.sum(q.astype(F32) * c_new, axis=-1, keepdims=True)
             + jnp.sum(qr.astype(F32) * krnew_ref[0].astype(F32), axis=-1, keepdims=True)) * SOFTMAX_LOG2_SCALE
    m_new = jnp.maximum(m, s_new)
    alpha = jnp.exp2(m - m_new)
    p_new = jnp.exp2(s_new - m_new)
    l = alpha * l + p_new
    acc = alpha * acc + p_new.astype(BF16).astype(F32) * c_new
    o_ref[0] = (acc / l).astype(BF16)


def _decode_attn(page_table, q_lat, q_rope, c_new, kr_new, cache_c, cache_krt, layer):
    n_seq, n_pages = page_table.shape
    group = 16
    while n_pages % (group * DECODE_SLOTS):
        group //= 2
    assert group >= 1, "pages per sequence must be a multiple of the ring depth"
    grid_spec = pltpu.PrefetchScalarGridSpec(
        num_scalar_prefetch=1,
        grid=(n_seq,),
        in_specs=[pl.BlockSpec((1, HEADS, KV_LORA), lambda n, pt: (n, 0, 0)),
                  pl.BlockSpec((1, HEADS, QK_ROPE), lambda n, pt: (n, 0, 0)),
                  pl.BlockSpec((1, 1, KV_LORA), lambda n, pt: (n, 0, 0)),
                  pl.BlockSpec((1, 1, QK_ROPE), lambda n, pt: (n, 0, 0)),
                  pl.BlockSpec(memory_space=pl.ANY),
                  pl.BlockSpec(memory_space=pl.ANY)],
        out_specs=pl.BlockSpec((1, HEADS, KV_LORA), lambda n, pt: (n, 0, 0)),
        scratch_shapes=[pltpu.VMEM((DECODE_SLOTS, group, PAGE, KV_LORA), F32),
                        pltpu.VMEM((DECODE_SLOTS, QK_ROPE, group * PAGE), F32),
                        pltpu.SemaphoreType.DMA((2, DECODE_SLOTS))],
    )
    return pl.pallas_call(
        functools.partial(_decode_attn_kernel, layer=layer, n_seq=n_seq, n_pages=n_pages, group=group),
        grid_spec=grid_spec,
        out_shape=jax.ShapeDtypeStruct((n_seq, HEADS, KV_LORA), BF16),
        compiler_params=_params(1, 48),
        name="decode_attn",
    )(page_table.reshape(-1), q_lat, q_rope, c_new, kr_new, cache_c, cache_krt)


def _ssm_conv_kernel(x_ref, halo_ref, w_ref, b_ref, o_ref, xp_s):
    tm = x_ref.shape[0]
    halo = halo_ref[...]
    xp_s[0:SUBLANES, :] = jnp.where(pl.program_id(0) == 0, jnp.zeros_like(halo), halo)
    xp_s[SUBLANES:, :] = x_ref[...]
    w = w_ref[...]
    base = SUBLANES - (SSM_CONV - 1)
    y = xp_s[base:base + tm, :] * w[0:1]
    for k in range(1, SSM_CONV):
        y = y + xp_s[base + k:base + k + tm, :] * w[k:k + 1]
    o_ref[...] = _silu(y + b_ref[...])


def _ssm_conv(proj, w, b):
    t = proj.shape[0]
    tm, tn = min(t, 512), 512
    c0 = P_XBC // tn
    return pl.pallas_call(
        _ssm_conv_kernel,
        grid=(t // tm, CONV_DIM // tn),
        in_specs=[pl.BlockSpec((tm, tn), lambda i, j: (i, c0 + j)),
                  pl.BlockSpec((SUBLANES, tn), lambda i, j: (jnp.maximum(i * (tm // SUBLANES) - 1, 0), c0 + j)),
                  pl.BlockSpec((SSM_CONV, tn), lambda i, j: (0, j)),
                  pl.BlockSpec((1, tn), lambda i, j: (0, j))],
        out_specs=pl.BlockSpec((tm, tn), lambda i, j: (i, j)),
        out_shape=jax.ShapeDtypeStruct((t, CONV_DIM), F32),
        scratch_shapes=[pltpu.VMEM((tm + SUBLANES, tn), F32)],
        compiler_params=_params(2, 32),
        name="ssm_conv",
    )(proj, proj, w, b)


def _ssm_conv_step_kernel(st_ref, x_ref, w_ref, b_ref, o_ref):
    w = w_ref[...]
    y = st_ref[0] * w[0:1]
    for k in range(1, SSM_CONV - 1):
        y = y + st_ref[k] * w[k:k + 1]
    y = y + x_ref[...] * w[SSM_CONV - 1:SSM_CONV]
    o_ref[...] = _silu(y + b_ref[...])


def _ssm_conv_step(state_t, proj, w, b):
    n = proj.shape[0]
    tn = 512
    c0 = P_XBC // tn
    return pl.pallas_call(
        _ssm_conv_step_kernel,
        grid=(CONV_DIM // tn,),
        in_specs=[pl.BlockSpec((SSM_CONV - 1, n, tn), lambda j: (0, 0, j)),
                  pl.BlockSpec((n, tn), lambda j: (0, c0 + j)),
                  pl.BlockSpec((SSM_CONV, tn), lambda j: (0, j)),
                  pl.BlockSpec((1, tn), lambda j: (0, j))],
        out_specs=pl.BlockSpec((n, tn), lambda j: (0, j)),
        out_shape=jax.ShapeDtypeStruct((n, CONV_DIM), F32),
        compiler_params=_params(1, 32),
        name="ssm_conv_step",
    )(state_t, proj, w, b)


def _ssd_chunk_kernel(xbc_ref, sm_ref, z_ref, dtb_ref, a_ref, dsk_ref, nw_ref, o_ref, hout_ref, st_s):
    c = pl.program_id(0)

    @pl.when(c == 0)
    def _():
        st_s[...] = jnp.zeros(st_s.shape, F32)

    ln = CHUNK
    dt = _softplus(sm_ref[:, LANES:2 * LANES] + dtb_ref[...])
    row = lax.broadcasted_iota(jnp.int32, (ln, ln), 0)
    col = lax.broadcasted_iota(jnp.int32, (ln, ln), 1)
    tri = row >= col
    cs = _dot_exact(tri.astype(F32), dt * a_ref[...])
    cs_t = cs.T
    dt_t = dt.T
    dtw_t = dt_t * jnp.exp(cs_t[:, ln - 1:ln] - cs_t)
    e_last = jnp.exp(cs[ln - 1:ln, :])
    lo_lane = lax.broadcasted_iota(jnp.int32, (ln, LANES), 1) < SSM_P
    lo_lane_row = lax.broadcasted_iota(jnp.int32, (1, LANES), 1) < SSM_P
    gw = D_INNER // SSM_GROUPS

    for g in range(SSM_GROUPS):
        b_f = xbc_ref[:, D_INNER + g * SSM_N:D_INNER + (g + 1) * SSM_N]
        c_f = xbc_ref[:, D_INNER + GN + g * SSM_N:D_INNER + GN + (g + 1) * SSM_N]
        cbt = _dot_nt(c_f.astype(BF16), b_f.astype(BF16))
        b_t = b_f.T
        gated = []
        for kk in range(PAIRS_PER_GROUP):
            k = g * PAIRS_PER_GROUP + kk
            sl = slice(k * LANES, (k + 1) * LANES)
            x2 = xbc_ref[:, sl]
            x2b = x2.astype(BF16)
            s_prev = st_s[k]
            rhs = jnp.concatenate([x2b, s_prev.astype(BF16)], axis=0)
            ys, sts = [], []
            for h in (2 * k, 2 * k + 1):
                cs_col = jnp.broadcast_to(cs[:, h:h + 1], (ln, ln))
                decay = jnp.exp(jnp.where(tri, cs_col - cs_t[h:h + 1, :], -jnp.inf))
                within = (cbt * decay * dt_t[h:h + 1, :]).astype(BF16)
                carried = (c_f * jnp.exp(cs_col)).astype(BF16)
                ys.append(_dot(jnp.concatenate([within, carried], axis=1), rhs))
                sts.append(_dot((b_t * dtw_t[h:h + 1, :]).astype(BF16), x2b))
            y = jnp.where(lo_lane, ys[0], ys[1]) + dsk_ref[:, sl] * x2
            keep = jnp.where(lo_lane_row, e_last[:, 2 * k:2 * k + 1], e_last[:, 2 * k + 1:2 * k + 2])
            st_s[k] = s_prev * keep + jnp.where(lo_lane, sts[0], sts[1])
            gated.append(y * _silu(z_ref[:, sl]))
        ssq = sum(jnp.sum(v * v, axis=-1, keepdims=True) for v in gated)
        scale = lax.rsqrt(ssq / gw + EPS)
        for kk, v in enumerate(gated):
            sl = slice((g * PAIRS_PER_GROUP + kk) * LANES, (g * PAIRS_PER_GROUP + kk + 1) * LANES)
            o_ref[:, sl] = (v * scale * nw_ref[:, sl]).astype(BF16)

    @pl.when(c == pl.num_programs(0) - 1)
    def _():
        for k in range(PAIRS):
            hout_ref[k] = st_s[k].T


def _ssd_chunked(xbc_act, proj, dt_bias_pad, a_pad, d_skip_cols, norm_w):
    t = xbc_act.shape[0]
    return pl.pallas_call(
        _ssd_chunk_kernel,
        grid=(t // CHUNK,),
        in_specs=[pl.BlockSpec((CHUNK, CONV_DIM), lambda c: (c, 0)),
                  pl.BlockSpec((CHUNK, SMALL_W), lambda c: (c, P_SMALL // SMALL_W)),
                  pl.BlockSpec((CHUNK, D_INNER), lambda c: (c, P_Z // D_INNER)),
                  pl.BlockSpec((1, LANES), lambda c: (0, 0)),
                  pl.BlockSpec((1, LANES), lambda c: (0, 0)),
                  pl.BlockSpec((1, D_INNER), lambda c: (0, 0)),
                  pl.BlockSpec((1, D_INNER), lambda c: (0, 0))],
        out_specs=[pl.BlockSpec((CHUNK, D_INNER), lambda c: (c, 0)),
                   pl.BlockSpec((PAIRS, LANES, SSM_N), lambda c: (0, 0, 0))],
        out_shape=[jax.ShapeDtypeStruct((t, D_INNER), BF16),
                   jax.ShapeDtypeStruct((PAIRS, LANES, SSM_N), F32)],
        scratch_shapes=[pltpu.VMEM((PAIRS, SSM_N, LANES), F32)],
        compiler_params=_params(1, 32),
        name="ssd_chunked",
    )(xbc_act, proj, proj, dt_bias_pad, a_pad, d_skip_cols, norm_w)


def _dt_prep_kernel(sm_ref, dtb_ref, a_ref, dt_ref, decay_ref):
    dt = _softplus(sm_ref[:, LANES:2 * LANES] + dtb_ref[...])
    dt_ref[...] = dt
    decay_ref[...] = jnp.exp(dt * a_ref[...])


def _dt_prep(proj, dt_bias_pad, a_pad):
    n = proj.shape[0]
    return pl.pallas_call(
        _dt_prep_kernel,
        grid=(1,),
        in_specs=[pl.BlockSpec((n, SMALL_W), lambda i: (0, P_SMALL // SMALL_W)),
                  pl.BlockSpec((1, LANES), lambda i: (0, 0)),
                  pl.BlockSpec((1, LANES), lambda i: (0, 0))],
        out_specs=[pl.BlockSpec((n, LANES), lambda i: (0, 0)), pl.BlockSpec((n, LANES), lambda i: (0, 0))],
        out_shape=[jax.ShapeDtypeStruct((n, LANES), F32), jax.ShapeDtypeStruct((n, LANES), F32)],
        compiler_params=_params(1, 32),
        name="dt_prep",
    )(proj, dt_bias_pad, a_pad)


def _ssd_step_kernel(decay_ref, st_ref, x_ref, b_ref, c_ref, dt_ref, dsk_ref, y_ref, so_ref):
    k = pl.program_id(0)
    n = x_ref.shape[0]
    head = lax.broadcasted_iota(jnp.int32, (LANES, LANES), 0)
    lane = lax.broadcasted_iota(jnp.int32, (LANES, LANES), 1)
    pick = (head == 2 * k + (lane >= SSM_P).astype(jnp.int32)).astype(F32)
    dt2 = _dot_exact(dt_ref[...], pick)
    x2 = x_ref[...]
    xd_t = (x2 * dt2).T
    c_t = c_ref[...].T.astype(BF16)
    ycols = jnp.zeros((LANES, n), F32)
    lane_n = lax.broadcasted_iota(jnp.int32, (LANES, n), 1)
    for t in range(n):
        upd = xd_t[:, t:t + 1] * b_ref[t:t + 1, :]
        new = []
        for hh in range(2):
            decay = decay_ref[t * SSM_HEADS + 2 * k + hh]
            new.append(st_ref[t, hh] * decay + upd[hh * SSM_P:(hh + 1) * SSM_P, :])
            so_ref[t, hh] = new[hh]
        y_all = _dot(jnp.concatenate(new, axis=0).astype(BF16), c_t)
        ycols = jnp.where(lane_n == t, y_all, ycols)
    y_ref[...] = ycols.T + dsk_ref[...] * x2


def _ssd_step(state, xbc_act, dt, decay, d_skip_cols):
    n = xbc_act.shape[0]
    assert n == LANES, "the per-sample SSD step transposes one (128, 128) tile of tokens"
    b0 = D_INNER // SSM_N
    c0 = (D_INNER + GN) // SSM_N
    st_spec = pl.BlockSpec((n, 2, SSM_P, SSM_N), lambda k: (0, k, 0, 0))
    return pl.pallas_call(
        _ssd_step_kernel,
        grid=(PAIRS,),
        in_specs=[pl.BlockSpec(memory_space=pltpu.SMEM),
                  st_spec,
                  pl.BlockSpec((n, LANES), lambda k: (0, k)),
                  pl.BlockSpec((n, SSM_N), lambda k: (0, b0 + k // PAIRS_PER_GROUP)),
                  pl.BlockSpec((n, SSM_N), lambda k: (0, c0 + k // PAIRS_PER_GROUP)),
                  pl.BlockSpec((n, LANES), lambda k: (0, 0)),
                  pl.BlockSpec((1, LANES), lambda k: (0, k))],
        out_specs=[pl.BlockSpec((n, LANES), lambda k: (0, k)), st_spec],
        out_shape=[jax.ShapeDtypeStruct((n, D_INNER), F32),
                   jax.ShapeDtypeStruct(state.shape, F32)],
        compiler_params=_params(1, 48),
        name="ssd_step",
    )(decay[:, :SSM_HEADS].reshape(-1), state, xbc_act, xbc_act, xbc_act, dt, d_skip_cols)


def _gate_norm_kernel(y_ref, z_ref, w_ref, o_ref):
    gw = D_INNER // SSM_GROUPS
    for g in range(SSM_GROUPS):
        sl = slice(g * gw, (g + 1) * gw)
        v = y_ref[:, sl] * _silu(z_ref[:, sl])
        o_ref[:, sl] = _rms(v, w_ref[:, sl]).astype(BF16)


def _gate_norm(y, proj, w):
    t = y.shape[0]
    tm = min(t, 512)
    return pl.pallas_call(
        _gate_norm_kernel,
        grid=(t // tm,),
        in_specs=[pl.BlockSpec((tm, D_INNER), lambda i: (i, 0)),
                  pl.BlockSpec((tm, D_INNER), lambda i: (i, P_Z // D_INNER)),
                  pl.BlockSpec((1, D_INNER), lambda i: (0, 0))],
        out_specs=pl.BlockSpec((tm, D_INNER), lambda i: (i, 0)),
        out_shape=jax.ShapeDtypeStruct((t, D_INNER), BF16),
        compiler_params=_params(1, 48),
        name="gate_norm",
    )(y, proj, w)


def _merge_kernel(oa_ref, os_ref, wa_ref, ws_ref, ga_ref, gs_ref, o_ref):
    a = _dot(oa_ref[...], wa_ref[...])
    s = _dot(os_ref[...], ws_ref[...])
    o_ref[...] = (jax.nn.sigmoid(ga_ref[...]) * a + jax.nn.sigmoid(gs_ref[...]) * s).astype(BF16)


def _merge(o_attn, o_ssm, w_attn, w_ssm, proj):
    t = o_attn.shape[0]
    tm, tn = min(t, 512), 512
    return pl.pallas_call(
        _merge_kernel,
        grid=(t // tm, D_MODEL // tn),
        in_specs=[pl.BlockSpec((tm, HEADS * V_DIM), lambda i, j: (i, 0)),
                  pl.BlockSpec((tm, D_INNER), lambda i, j: (i, 0)),
                  pl.BlockSpec((HEADS * V_DIM, tn), lambda i, j: (0, j)),
                  pl.BlockSpec((D_INNER, tn), lambda i, j: (0, j)),
                  pl.BlockSpec((tm, tn), lambda i, j: (i, P_GA // tn + j)),
                  pl.BlockSpec((tm, tn), lambda i, j: (i, P_GS // tn + j))],
        out_specs=pl.BlockSpec((tm, tn), lambda i, j: (i, j)),
        out_shape=jax.ShapeDtypeStruct((t, D_MODEL), BF16),
        compiler_params=_params(2, 48),
        name="merge",
    )(o_attn, o_ssm, w_attn, w_ssm, proj, proj)


def _out_proj_kernel(m_ref, w_ref, h_ref, npost_ref, npre_ref, h1_ref, hn_ref):
    h1 = h_ref[...] + _rms(_dot(m_ref[...], w_ref[...]), npost_ref[...])
    h1_ref[...] = h1
    hn_ref[...] = _rms(h1, npre_ref[...]).astype(BF16)


def _out_proj(merged, w_out, h, n_post, n_pre):
    t = h.shape[0]
    tm = min(t, 256)
    row = lambda i: (i, 0)
    fixed = lambda i: (0, 0)
    return pl.pallas_call(
        _out_proj_kernel,
        grid=(t // tm,),
        in_specs=[pl.BlockSpec((tm, D_MODEL), row), pl.BlockSpec((D_MODEL, D_MODEL), fixed),
                  pl.BlockSpec((tm, D_MODEL), row), pl.BlockSpec((1, D_MODEL), fixed),
                  pl.BlockSpec((1, D_MODEL), fixed)],
        out_specs=[pl.BlockSpec((tm, D_MODEL), row), pl.BlockSpec((tm, D_MODEL), row)],
        out_shape=[jax.ShapeDtypeStruct((t, D_MODEL), F32), jax.ShapeDtypeStruct((t, D_MODEL), BF16)],
        compiler_params=_params(1, 48),
        name="out_proj",
    )(merged, w_out, h, n_post, n_pre)


FFN_HALO = BF16_SUBLANES
FFN_ROW_CHUNKS = 2


def _ffn_kernel(x_ref, halo_ref, wg_ref, wv_ref, cwg_ref, cwv_ref, cbg_ref, cbv_ref, wd_ref, h_ref, nw_ref,
                o_ref, tg_ref, tv_ref, xe_s, u_s, acc_s):
    i, j = pl.program_id(0), pl.program_id(1)
    tm, tf = x_ref.shape[0], wg_ref.shape[1]

    @pl.when(j == 0)
    def _():
        halo = halo_ref[...]
        xe_s[0:FFN_HALO, :] = jnp.where(i == 0, jnp.zeros_like(halo), halo)
        xe_s[FFN_HALO:, :] = x_ref[...]
        acc_s[...] = jnp.zeros(acc_s.shape, F32)

    base = FFN_HALO - (FFN_CONV - 1)
    rc = tm // FFN_ROW_CHUNKS
    for c in range(FFN_ROW_CHUNKS):
        r0 = 0 if c == 0 else FFN_HALO + c * rc
        r1 = FFN_HALO + (c + 1) * rc
        xe = xe_s[r0:r1, :]
        u_s[r0:r1, 0:tf] = _dot(xe, wg_ref[...])
        u_s[r0:r1, tf:] = _dot(xe, wv_ref[...])
        t0 = base + c * rc

        def conv(lo, w_ref, b_ref):
            w = w_ref[...]
            y = u_s[t0:t0 + rc, lo:lo + tf] * w[0:1]
            for k in range(1, FFN_CONV):
                y = y + u_s[t0 + k:t0 + k + rc, lo:lo + tf] * w[k:k + 1]
            return y + b_ref[...]

        f = (_gelu_tanh(conv(0, cwg_ref, cbg_ref)) * conv(tf, cwv_ref, cbv_ref)).astype(BF16)
        acc_s[c * rc:(c + 1) * rc, :] += _dot(f, wd_ref[...])

    tail = FFN_HALO + tm - SUBLANES
    tg_ref[...] = u_s[tail:tail + SUBLANES, 0:tf]
    tv_ref[...] = u_s[tail:tail + SUBLANES, tf:]

    @pl.when(j == pl.num_programs(1) - 1)
    def _():
        o_ref[...] = h_ref[...] + _rms(acc_s[...], nw_ref[...])


def _ffn(hn, h, w_up, conv_w, conv_b, w_down, nw):
    t = h.shape[0]
    tm, tf = min(t, 512), 512
    nj = D_FF // tf
    gate = lambda i, j: (0, j)
    value = lambda i, j: (0, nj + j)
    row = lambda i, j: (i, 0)
    return pl.pallas_call(
        _ffn_kernel,
        grid=(t // tm, nj),
        in_specs=[pl.BlockSpec((tm, D_MODEL), row),
                  pl.BlockSpec((FFN_HALO, D_MODEL), lambda i, j: (jnp.maximum(i * (tm // FFN_HALO) - 1, 0), 0)),
                  pl.BlockSpec((D_MODEL, tf), gate), pl.BlockSpec((D_MODEL, tf), value),
                  pl.BlockSpec((FFN_CONV, tf), gate), pl.BlockSpec((FFN_CONV, tf), value),
                  pl.BlockSpec((1, tf), gate), pl.BlockSpec((1, tf), value),
                  pl.BlockSpec((tf, D_MODEL), lambda i, j: (j, 0)),
                  pl.BlockSpec((tm, D_MODEL), row),
                  pl.BlockSpec((1, D_MODEL), lambda i, j: (0, 0))],
        out_specs=[pl.BlockSpec((tm, D_MODEL), row),
                   pl.BlockSpec((SUBLANES, tf), lambda i, j: (i, j)),
                   pl.BlockSpec((SUBLANES, tf), lambda i, j: (i, j))],
        out_shape=[jax.ShapeDtypeStruct((t, D_MODEL), F32),
                   jax.ShapeDtypeStruct((t // tm * SUBLANES, D_FF), F32),
                   jax.ShapeDtypeStruct((t // tm * SUBLANES, D_FF), F32)],
        scratch_shapes=[pltpu.VMEM((tm + FFN_HALO, D_MODEL), BF16),
                        pltpu.VMEM((tm + FFN_HALO, 2 * tf), F32),
                        pltpu.VMEM((tm, D_MODEL), F32)],
        compiler_params=_params(2, 56),
        name="ffn",
    )(hn, hn, w_up, w_up, conv_w, conv_w, conv_b, conv_b, w_down, h, nw)


def _ffn_step_kernel(x_ref, wg_ref, wv_ref, sg_ref, sv_ref, cwg_ref, cwv_ref, cbg_ref, cbv_ref, wd_ref, h_ref,
                     nw_ref, o_ref, ug_ref, uv_ref, acc_s):
    j = pl.program_id(0)
    x = x_ref[...]

    def conv(u, s_ref, w_ref, b_ref):
        w = w_ref[...]
        y = s_ref[0] * w[0:1]
        for k in range(1, FFN_CONV - 1):
            y = y + s_ref[k] * w[k:k + 1]
        return y + u * w[FFN_CONV - 1:FFN_CONV] + b_ref[...]

    ug = _dot(x, wg_ref[...])
    uv = _dot(x, wv_ref[...])
    ug_ref[...] = ug
    uv_ref[...] = uv
    f = (_gelu_tanh(conv(ug, sg_ref, cwg_ref, cbg_ref)) * conv(uv, sv_ref, cwv_ref, cbv_ref)).astype(BF16)
    d = _dot(f, wd_ref[...])

    @pl.when(j == 0)
    def _():
        acc_s[...] = d

    @pl.when(j > 0)
    def _():
        acc_s[...] += d

    @pl.when(j == pl.num_programs(0) - 1)
    def _():
        o_ref[...] = h_ref[...] + _rms(acc_s[...], nw_ref[...])


def _ffn_step(hn, h, state_t, w_up, conv_w, conv_b, w_down, nw):
    n = h.shape[0]
    tf = 512
    nj = D_FF // tf
    gate = lambda j: (0, j)
    value = lambda j: (0, nj + j)
    fixed = lambda j: (0, 0)
    return pl.pallas_call(
        _ffn_step_kernel,
        grid=(nj,),
        in_specs=[pl.BlockSpec((n, D_MODEL), fixed),
                  pl.BlockSpec((D_MODEL, tf), gate), pl.BlockSpec((D_MODEL, tf), value),
                  pl.BlockSpec((FFN_CONV - 1, n, tf), lambda j: (0, 0, j)),
                  pl.BlockSpec((FFN_CONV - 1, n, tf), lambda j: (0, 0, nj + j)),
                  pl.BlockSpec((FFN_CONV, tf), gate), pl.BlockSpec((FFN_CONV, tf), value),
                  pl.BlockSpec((1, tf), gate), pl.BlockSpec((1, tf), value),
                  pl.BlockSpec((tf, D_MODEL), lambda j: (j, 0)),
                  pl.BlockSpec((n, D_MODEL), fixed),
                  pl.BlockSpec((1, D_MODEL), fixed)],
        out_specs=[pl.BlockSpec((n, D_MODEL), fixed), pl.BlockSpec((n, tf), gate), pl.BlockSpec((n, tf), gate)],
        out_shape=[jax.ShapeDtypeStruct((n, D_MODEL), F32),
                   jax.ShapeDtypeStruct((n, D_FF), F32), jax.ShapeDtypeStruct((n, D_FF), F32)],
        scratch_shapes=[pltpu.VMEM((n, D_MODEL), F32)],
        compiler_params=_params(1, 32),
        name="ffn_step",
    )(hn, w_up, w_up, state_t, state_t, conv_w, conv_w, conv_b, conv_b, w_down, h, nw)


def _ple_kernel(h_ref, pe_ref, wg_ref, wp_ref, nw_ref, o_ref):
    h = h_ref[...]
    gate = jax.nn.sigmoid(_dot(h.astype(BF16), wg_ref[...]))
    o_ref[...] = h + _rms(gate * _dot(pe_ref[...].astype(BF16), wp_ref[...]), nw_ref[...])


def _ple(h, pe, w_gate, w_proj, nw):
    t = h.shape[0]
    tm = min(t, 256)
    row = lambda i: (i, 0)
    fixed = lambda i: (0, 0)
    return pl.pallas_call(
        _ple_kernel,
        grid=(t // tm,),
        in_specs=[pl.BlockSpec((tm, D_MODEL), row), pl.BlockSpec((tm, PLE_DIM), row),
                  pl.BlockSpec((D_MODEL, D_MODEL), fixed), pl.BlockSpec((PLE_DIM, D_MODEL), fixed),
                  pl.BlockSpec((1, D_MODEL), fixed)],
        out_specs=pl.BlockSpec((tm, D_MODEL), row),
        out_shape=jax.ShapeDtypeStruct((t, D_MODEL), F32),
        compiler_params=_params(1, 48),
        name="ple",
    )(h, pe, w_gate, w_proj, nw)


def _rot_half_cols(w):
    half = QK_ROPE // 2
    return jnp.concatenate([-w[..., half:], w[..., :half]], axis=-1)


def _prepare_weights(w):
    w_in_t = w['w_in'].T
    splits = np.cumsum([0, Q_LORA, KV_LORA, QK_ROPE, D_INNER, CONV_DIM, SSM_HEADS, D_MODEL, D_MODEL])
    q_c, kv_c, k_r, z, xbc, dt, g_a, g_s = [w_in_t[a:b] for a, b in zip(splits[:-1], splits[1:])]
    pad = jnp.zeros((SMALL_W - 2 * QK_ROPE - SSM_HEADS, D_MODEL), F32)
    k_r_rot = _rot_half_cols(k_r.T).T
    packed_t = jnp.concatenate([z, g_a, g_s, xbc, kv_c, k_r, k_r_rot, dt, pad, q_c], axis=0)
    wq = w['w_q_up'].reshape(Q_LORA, HEADS, QK_NOPE + QK_ROPE)
    wq_rope = wq[:, :, QK_NOPE:]
    wkv = w['w_kv_up'].reshape(KV_LORA, HEADS, QK_NOPE + V_DIM)
    lane_pad = LANES - SSM_HEADS
    return dict(
        w_in=packed_t.astype(BF16),
        wq_nope=wq[:, :, :QK_NOPE].reshape(Q_LORA, HEADS * QK_NOPE).astype(BF16),
        wq_rope=wq_rope.reshape(Q_LORA, HEADS * QK_ROPE).astype(BF16),
        wq_rope_rot=_rot_half_cols(wq_rope).reshape(Q_LORA, HEADS * QK_ROPE).astype(BF16),
        w_nope=wkv[:, :, :QK_NOPE].transpose(1, 2, 0).astype(BF16),
        w_v=wkv[:, :, QK_NOPE:].transpose(1, 0, 2).astype(BF16),
        w_branch_attn=w['w_branch_attn'].astype(BF16),
        w_branch_ssm=w['w_branch_ssm'].astype(BF16),
        w_out=w['w_out'].astype(BF16),
        w_ffn_up=w['w_ffn_up'].astype(BF16),
        w_ffn_down=w['w_ffn_down'].astype(BF16),
        w_ple_gate=w['w_ple_gate'].astype(BF16),
        w_ple_proj=w['w_ple_proj'].astype(BF16),
        dt_bias=jnp.pad(w['dt_bias'], (0, lane_pad))[None, :],
        a_neg=jnp.pad(-jnp.exp(w['a_log']), (0, lane_pad))[None, :],
        d_skip=jnp.repeat(w['d_skip'], SSM_P)[None, :],
    )


def _rope_tables(pos):
    half = QK_ROPE // 2
    inv = ROPE_THETA ** (-jnp.arange(half, dtype=F32) / half)
    ang = pos[:, None] * inv[None, :]
    cos, sin = jnp.cos(ang), jnp.sin(ang)
    cos64, sin64 = jnp.concatenate([cos, cos], axis=1), jnp.concatenate([sin, sin], axis=1)
    return cos64, sin64, jnp.concatenate([cos64, cos64], axis=1), jnp.concatenate([sin64, sin64], axis=1)


def _vec(v):
    return v[None, :]


def _mixer_front(h, pos, w, pw):
    proj = _in_proj(h, _vec(w['norm_pre_mix']), pw['w_in'])
    cos64, sin64, cos128, sin128 = _rope_tables(pos)
    c, c_bf, kr, kr_bf = _kv_prep(proj, cos64, sin64, _vec(w['norm_kv']))
    q_nope, q_rope = _q_prep(proj, _vec(w['norm_q']), pw['wq_nope'], pw['wq_rope'], pw['wq_rope_rot'],
                             cos128, sin128)
    return proj, c, c_bf, kr, kr_bf, q_nope, q_rope


def _mixer_back(h, proj, o_attn, o_ssm, w, pw):
    merged = _merge(o_attn, o_ssm, pw['w_branch_attn'], pw['w_branch_ssm'], proj)
    return _out_proj(merged, pw['w_out'], h, _vec(w['norm_post_mix']), _vec(w['norm_pre_ffn']))


def _prompt_layer(h, pe, w, pw):
    t = h.shape[0]
    pos = jnp.arange(t, dtype=F32)
    proj, c, c_bf, kr, kr_bf, q_nope, q_rope = _mixer_front(h, pos, w, pw)
    o_attn = _prompt_attn(q_nope, q_rope, pw['w_nope'], pw['w_v'], c_bf, kr_bf)
    xbc_act = _ssm_conv(proj, w['conv_ssm_w'], _vec(w['conv_ssm_b']))
    o_ssm, st = _ssd_chunked(xbc_act, proj, pw['dt_bias'], pw['a_neg'], pw['d_skip'], _vec(w['norm_ssm']))
    h1, hn1 = _mixer_back(h, proj, o_attn, o_ssm, w, pw)
    h2, tail_g, tail_v = _ffn(hn1, h1, pw['w_ffn_up'], w['conv_ffn_w'], _vec(w['conv_ffn_b']),
                              pw['w_ffn_down'], _vec(w['norm_post_ffn']))
    h3 = _ple(h2, pe, pw['w_ple_gate'], pw['w_ple_proj'], _vec(w['norm_ple']))
    ssm_conv_new = proj[t - (SSM_CONV - 1):, P_XBC:P_XBC + CONV_DIM]
    last = tail_g.shape[0] - (FFN_CONV - 1)
    ffn_conv_new = jnp.concatenate([tail_g[last:], tail_v[last:]], axis=1)
    return h3, (c, kr, st.reshape(SSM_HEADS, SSM_P, SSM_N), ssm_conv_new, ffn_conv_new)


def _sample_layer(h, pe, past, w, pw, cache_c, cache_kr, layer, page_table, ssm_state, ssm_conv_state,
                  ffn_conv_state):
    n = h.shape[0]
    pos = jnp.full((n,), past, F32)
    proj, c, c_bf, kr, kr_bf, q_nope, q_rope = _mixer_front(h, pos, w, pw)
    q_lat = _q_latent_sample(q_nope, pw['w_nope']).transpose(1, 0, 2)
    o_lat = _decode_attn(page_table, q_lat, q_rope.reshape(n, HEADS, QK_ROPE), c_bf[:, None, :],
                         kr_bf[:, None, :], cache_c, jnp.swapaxes(cache_kr, 2, 3), layer)
    o_attn = _o_attn_sample(o_lat.transpose(1, 0, 2), pw['w_v'])
    xbc_act = _ssm_conv_step(ssm_conv_state.transpose(1, 0, 2), proj, w['conv_ssm_w'], _vec(w['conv_ssm_b']))
    dt, decay = _dt_prep(proj, pw['dt_bias'], pw['a_neg'])
    y_ssm, st = _ssd_step(ssm_state, xbc_act, dt, decay, pw['d_skip'])
    o_ssm = _gate_norm(y_ssm, proj, _vec(w['norm_ssm']))
    h1, hn1 = _mixer_back(h, proj, o_attn, o_ssm, w, pw)
    h2, u_g, u_v = _ffn_step(hn1, h1, ffn_conv_state.transpose(1, 0, 2), pw['w_ffn_up'], w['conv_ffn_w'],
                             _vec(w['conv_ffn_b']), pw['w_ffn_down'], _vec(w['norm_post_ffn']))
    h3 = _ple(h2, pe, pw['w_ple_gate'], pw['w_ple_proj'], _vec(w['norm_ple']))
    xbc_raw = proj[:, P_XBC:P_XBC + CONV_DIM]
    ssm_conv_new = jnp.concatenate([ssm_conv_state[:, 1:], xbc_raw[:, None, :]], axis=1)
    u = jnp.concatenate([u_g, u_v], axis=1)
    ffn_conv_new = jnp.concatenate([ffn_conv_state[:, 1:], u[:, None, :]], axis=1)
    return h3, (c[:, None, :], kr[:, None, :], st, ssm_conv_new, ffn_conv_new)


def kernel(x_prompt, x_sample, cache_kv_latent, cache_k_rope, state_ssm, state_ssm_conv, state_ffn_conv, page_table, p_prompt, p_sample, norm_pre_mix, w_in, norm_q, w_q_up, norm_kv, w_kv_up, w_branch_attn, conv_ssm_w, conv_ssm_b, dt_bias, a_log, d_skip, norm_ssm, w_branch_ssm, w_out, norm_post_mix, norm_pre_ffn, w_ffn_up, conv_ffn_w, conv_ffn_b, w_ffn_down, norm_post_ffn, w_ple_gate, w_ple_proj, norm_ple):
    assert x_prompt.shape[0] == 1, "one prompt sequence"
    assert x_sample.shape[1] == 1, "one new token per sample sequence"
    depth = w_in.shape[0]
    past = page_table.shape[1] * PAGE
    weights = dict(norm_pre_mix=norm_pre_mix, w_in=w_in, norm_q=norm_q, w_q_up=w_q_up, norm_kv=norm_kv,
                   w_kv_up=w_kv_up, w_branch_attn=w_branch_attn, conv_ssm_w=conv_ssm_w, conv_ssm_b=conv_ssm_b,
                   dt_bias=dt_bias, a_log=a_log, d_skip=d_skip, norm_ssm=norm_ssm, w_branch_ssm=w_branch_ssm,
                   w_out=w_out, norm_post_mix=norm_post_mix, norm_pre_ffn=norm_pre_ffn, w_ffn_up=w_ffn_up,
                   conv_ffn_w=conv_ffn_w, conv_ffn_b=conv_ffn_b, w_ffn_down=w_ffn_down,
                   norm_post_ffn=norm_post_ffn, w_ple_gate=w_ple_gate, w_ple_proj=w_ple_proj, norm_ple=norm_ple)
    hp, hs = x_prompt[0], x_sample[:, 0]
    outs_p, outs_s = [], []
    for d in range(depth):
        w = {k: v[d] for k, v in weights.items()}
        pw = _prepare_weights(w)
        hp, st_p = _prompt_layer(hp, p_prompt[d, 0], w, pw)
        hs, st_s = _sample_layer(hs, p_sample[d, :, 0], past, w, pw, cache_kv_latent, cache_k_rope, d,
                                 page_table, state_ssm[d], state_ssm_conv[d], state_ffn_conv[d])
        outs_p.append(st_p)
        outs_s.append(st_s)
    stack_p = [jnp.stack([o[k] for o in outs_p])[:, None] for k in range(5)]
    stack_s = [jnp.stack([o[k] for o in outs_s]) for k in range(5)]
    return (hp[None], hs[:, None], *stack_p, *stack_s)
```

```python
import functools

import jax
import jax.numpy as jnp
import numpy as np
from jax import lax
from jax.experimental import pallas as pl
from jax.experimental.pallas import tpu as pltpu

F32, BF16 = jnp.float32, jnp.bfloat16
HIGHEST = lax.Precision.HIGHEST

LANES = 128
SUBLANES = 8
BF16_SUBLANES = 16
V7X_VMEM_BYTES = 64 * 1024 * 1024

D_MODEL = 2048
PAGE = 128
HEADS = 16
QK_NOPE = 128
QK_ROPE = 64
V_DIM = 128
Q_LORA = 768
KV_LORA = 512
ROPE_THETA = 10000.0
ATTN_SCALE = (QK_NOPE + QK_ROPE) ** -0.5
SOFTMAX_LOG2_SCALE = ATTN_SCALE * float(np.log2(np.e))
SSM_HEADS = 32
SSM_P = 64
D_INNER = SSM_HEADS * SSM_P
SSM_GROUPS = 4
SSM_N = 128
SSM_CONV = 4
GN = SSM_GROUPS * SSM_N
CONV_DIM = D_INNER + 2 * GN
CHUNK = 128
PAIRS = SSM_HEADS // 2
PAIRS_PER_GROUP = PAIRS // SSM_GROUPS
D_FF = 4 * D_MODEL
FFN_CONV = 3
PLE_DIM = 256
EPS = 1e-6

P_Z = 0
P_GA = P_Z + D_INNER
P_GS = P_GA + D_MODEL
P_XBC = P_GS + D_MODEL
P_KV = P_XBC + CONV_DIM
P_SMALL = P_KV + KV_LORA
P_Q = P_SMALL + 2 * LANES
P_COLS = P_Q + Q_LORA
SMALL_W = 2 * LANES


def _params(n_axes, vmem_mib):
    return pltpu.CompilerParams(dimension_semantics=("arbitrary",) * n_axes,
                                vmem_limit_bytes=vmem_mib * 1024 * 1024)


def _rms(x, w):
    return x * lax.rsqrt(jnp.mean(x * x, axis=-1, keepdims=True) + EPS) * w


def _dot(a, b):
    return jnp.dot(a, b, preferred_element_type=F32)


def _dot_nt(a, b):
    return lax.dot_general(a, b, (((1,), (1,)), ((), ())), preferred_element_type=F32)


def _dot_exact(a, b):
    return jnp.dot(a, b, precision=HIGHEST, preferred_element_type=F32)


def _rep(x, n):
    return x if n == 1 else jnp.concatenate([x] * n, axis=1)


def _silu(x):
    return x * jax.nn.sigmoid(x)


def _softplus(x):
    return jnp.maximum(x, 0.0) + jnp.log1p(jnp.exp(-jnp.abs(x)))


def _gelu_tanh(x):
    return x * (0.5 * (1.0 + jnp.tanh(np.sqrt(2.0 / np.pi).astype(np.float32) * (x + 0.044715 * (x * x * x)))))


def _in_proj_kernel(x_ref, nw_ref, w_ref, o_ref, xn_ref):
    @pl.when(pl.program_id(1) == 0)
    def _():
        xn_ref[...] = _rms(x_ref[...], nw_ref[...]).astype(BF16)

    o_ref[...] = _dot_nt(xn_ref[...], w_ref[...])


def _in_proj(x, nw, w_packed_t):
    t = x.shape[0]
    tm, tn = min(t, 1024), 1536
    return pl.pallas_call(
        _in_proj_kernel,
        grid=(t // tm, P_COLS // tn),
        in_specs=[pl.BlockSpec((tm, D_MODEL), lambda i, j: (i, 0)),
                  pl.BlockSpec((1, D_MODEL), lambda i, j: (0, 0)),
                  pl.BlockSpec((tn, D_MODEL), lambda i, j: (j, 0))],
        out_specs=pl.BlockSpec((tm, tn), lambda i, j: (i, j)),
        out_shape=jax.ShapeDtypeStruct((t, P_COLS), F32),
        scratch_shapes=[pltpu.VMEM((tm, D_MODEL), BF16)],
        compiler_params=_params(2, 56),
        name="in_proj",
    )(x, nw, w_packed_t)


def _kv_prep_kernel(kv_ref, sm_ref, cos_ref, sin_ref, nw_ref, c_ref, cbf_ref, kr_ref, krbf_ref):
    c = _rms(kv_ref[...], nw_ref[...])
    c_ref[...] = c
    cbf_ref[...] = c.astype(BF16)
    sm = sm_ref[...]
    kr = sm[:, 0:QK_ROPE] * cos_ref[...] + sm[:, QK_ROPE:2 * QK_ROPE] * sin_ref[...]
    kr_ref[...] = kr
    krbf_ref[...] = kr.astype(BF16)


def _kv_prep(proj, cos64, sin64, nw):
    t = proj.shape[0]
    tm = min(t, 512)
    return pl.pallas_call(
        _kv_prep_kernel,
        grid=(t // tm,),
        in_specs=[pl.BlockSpec((tm, KV_LORA), lambda i: (i, P_KV // KV_LORA)),
                  pl.BlockSpec((tm, SMALL_W), lambda i: (i, P_SMALL // SMALL_W)),
                  pl.BlockSpec((tm, QK_ROPE), lambda i: (i, 0)),
                  pl.BlockSpec((tm, QK_ROPE), lambda i: (i, 0)),
                  pl.BlockSpec((1, KV_LORA), lambda i: (0, 0))],
        out_specs=[pl.BlockSpec((tm, KV_LORA), lambda i: (i, 0)),
                   pl.BlockSpec((tm, KV_LORA), lambda i: (i, 0)),
                   pl.BlockSpec((tm, QK_ROPE), lambda i: (i, 0)),
                   pl.BlockSpec((tm, QK_ROPE), lambda i: (i, 0))],
        out_shape=[jax.ShapeDtypeStruct((t, KV_LORA), F32), jax.ShapeDtypeStruct((t, KV_LORA), BF16),
                   jax.ShapeDtypeStruct((t, QK_ROPE), F32), jax.ShapeDtypeStruct((t, QK_ROPE), BF16)],
        compiler_params=_params(1, 32),
        name="kv_prep",
    )(proj, proj, cos64, sin64, nw)


def _q_prep_kernel(q_ref, nw_ref, wn_ref, wr_ref, wrr_ref, cos_ref, sin_ref, qn_ref, qr_ref):
    qn = _rms(q_ref[...], nw_ref[...]).astype(BF16)
    qn_ref[...] = _dot(qn, wn_ref[...]).astype(BF16)
    r = _dot(qn, wr_ref[...])
    rr = _dot(qn, wrr_ref[...])
    cos, sin = cos_ref[...], sin_ref[...]
    for k in range(HEADS * QK_ROPE // LANES):
        sl = slice(k * LANES, (k + 1) * LANES)
        qr_ref[:, sl] = (r[:, sl] * cos + rr[:, sl] * sin).astype(BF16)


def _q_prep(proj, nw, wq_nope, wq_rope, wq_rope_rot, cos128, sin128):
    t = proj.shape[0]
    tm = min(t, 512)
    n_nope, n_rope = HEADS * QK_NOPE, HEADS * QK_ROPE
    return pl.pallas_call(
        _q_prep_kernel,
        grid=(t // tm,),
        in_specs=[pl.BlockSpec((tm, Q_LORA), lambda i: (i, P_Q // Q_LORA)),
                  pl.BlockSpec((1, Q_LORA), lambda i: (0, 0)),
                  pl.BlockSpec((Q_LORA, n_nope), lambda i: (0, 0)),
                  pl.BlockSpec((Q_LORA, n_rope), lambda i: (0, 0)),
                  pl.BlockSpec((Q_LORA, n_rope), lambda i: (0, 0)),
                  pl.BlockSpec((tm, LANES), lambda i: (i, 0)),
                  pl.BlockSpec((tm, LANES), lambda i: (i, 0))],
        out_specs=[pl.BlockSpec((tm, n_nope), lambda i: (i, 0)),
                   pl.BlockSpec((tm, n_rope), lambda i: (i, 0))],
        out_shape=[jax.ShapeDtypeStruct((t, n_nope), BF16), jax.ShapeDtypeStruct((t, n_rope), BF16)],
        compiler_params=_params(1, 48),
        name="q_prep",
    )(proj, nw, wq_nope, wq_rope, wq_rope_rot, cos128, sin128)


def _prompt_attn_kernel(qi_ref, kj_ref, qn_ref, qr_ref, wn_ref, wv_ref, c_ref, kr_ref, o_ref,
                        qlat_s, qrope_s, m_s, l_s, acc_s, *, tq):
    step = pl.program_id(0)
    i = qi_ref[step]
    j = kj_ref[step]
    j_last = i // 4
    r = i % 4

    @pl.when(j == 0)
    def _init():
        for h in range(HEADS):
            rs = slice(h * tq, (h + 1) * tq)
            qlat_s[rs, :] = _dot(qn_ref[:, h * QK_NOPE:(h + 1) * QK_NOPE], wn_ref[h]).astype(BF16)
            qrope_s[rs, :] = qr_ref[:, h * QK_ROPE:(h + 1) * QK_ROPE]
        m_s[...] = jnp.full(m_s.shape, -jnp.inf, F32)
        l_s[...] = jnp.zeros(l_s.shape, F32)
        acc_s[...] = jnp.zeros(acc_s.shape, F32)

    def scores(k0, nk):
        return (_dot_nt(qlat_s[...], c_ref[pl.ds(k0, nk), :])
                + _dot_nt(qrope_s[...], kr_ref[pl.ds(k0, nk), :])) * SOFTMAX_LOG2_SCALE

    def update(s, k0, nk, limit=None):
        if limit is not None:
            visible = (lax.broadcasted_iota(jnp.int32, (1, tq, nk), 2)
                       <= lax.broadcasted_iota(jnp.int32, (1, tq, nk), 1) + limit)
            s = jnp.where(visible, s.reshape(HEADS, tq, nk), -jnp.inf).reshape(HEADS * tq, nk)
        m_prev = m_s[...]
        m_new = jnp.maximum(m_prev, jnp.max(s, axis=-1, keepdims=True))
        alpha = jnp.exp2(m_prev - m_new)
        p = jnp.exp2(s - _rep(m_new, nk // LANES))
        l_s[...] = alpha * l_s[...] + jnp.sum(p, axis=-1, keepdims=True)
        acc_s[...] = acc_s[...] * _rep(alpha, KV_LORA // LANES) + _dot(p.astype(BF16), c_ref[pl.ds(k0, nk), :])
        m_s[...] = m_new

    @pl.when(j < j_last)
    def _below_diagonal():
        s_a = scores(0, 2 * tq)
        s_b = scores(2 * tq, 2 * tq)
        update(s_a, 0, 2 * tq)
        update(s_b, 2 * tq, 2 * tq)

    @pl.when(j == j_last)
    def _diagonal_block():
        def quarter(q, carry):
            k0 = pl.multiple_of(q * tq, tq)
            update(scores(k0, tq), k0, tq, limit=(r - q) * tq)
            return carry

        lax.fori_loop(0, r + 1, quarter, 0)

    @pl.when(j == j_last)
    def _finish():
        for h in range(HEADS):
            rs = slice(h * tq, (h + 1) * tq)
            o_lat = (acc_s[rs, :] / _rep(l_s[rs, :], KV_LORA // LANES)).astype(BF16)
            o_ref[:, h * V_DIM:(h + 1) * V_DIM] = _dot(o_lat, wv_ref[h]).astype(BF16)


def _prompt_attn(q_nope, q_rope, w_nope, w_v, c_bf, kr_bf):
    t = q_nope.shape[0]
    tq = 256
    tk = 4 * tq
    assert t % tk == 0, "prompt length must be a multiple of the key block"
    pairs = [(i, j) for i in range(t // tq) for j in range(i // 4 + 1)]
    qi = jnp.asarray([p[0] for p in pairs], jnp.int32)
    kj = jnp.asarray([p[1] for p in pairs], jnp.int32)
    rows = HEADS * tq
    grid_spec = pltpu.PrefetchScalarGridSpec(
        num_scalar_prefetch=2,
        grid=(len(pairs),),
        in_specs=[pl.BlockSpec((tq, HEADS * QK_NOPE), lambda s, qi, kj: (qi[s], 0)),
                  pl.BlockSpec((tq, HEADS * QK_ROPE), lambda s, qi, kj: (qi[s], 0)),
                  pl.BlockSpec((HEADS, QK_NOPE, KV_LORA), lambda s, qi, kj: (0, 0, 0)),
                  pl.BlockSpec((HEADS, KV_LORA, V_DIM), lambda s, qi, kj: (0, 0, 0)),
                  pl.BlockSpec((tk, KV_LORA), lambda s, qi, kj: (kj[s], 0)),
                  pl.BlockSpec((tk, QK_ROPE), lambda s, qi, kj: (kj[s], 0))],
        out_specs=pl.BlockSpec((tq, HEADS * V_DIM), lambda s, qi, kj: (qi[s], 0)),
        scratch_shapes=[pltpu.VMEM((rows, KV_LORA), BF16), pltpu.VMEM((rows, QK_ROPE), BF16),
                        pltpu.VMEM((rows, LANES), F32), pltpu.VMEM((rows, LANES), F32),
                        pltpu.VMEM((rows, KV_LORA), F32)],
    )
    return pl.pallas_call(
        functools.partial(_prompt_attn_kernel, tq=tq),
        grid_spec=grid_spec,
        out_shape=jax.ShapeDtypeStruct((t, HEADS * V_DIM), BF16),
        compiler_params=_params(1, 56),
        name="prompt_attn",
    )(qi, kj, q_nope, q_rope, w_nope, w_v, c_bf, kr_bf)


def _head_proj_in_kernel(x_ref, w_ref, o_ref):
    o_ref[0] = _dot(x_ref[...], w_ref[0]).astype(BF16)


def _q_latent_sample(q_nope, w_nope):
    n = q_nope.shape[0]
    return pl.pallas_call(
        _head_proj_in_kernel,
        grid=(HEADS,),
        in_specs=[pl.BlockSpec((n, QK_NOPE), lambda h: (0, h)),
                  pl.BlockSpec((1, QK_NOPE, KV_LORA), lambda h: (h, 0, 0))],
        out_specs=pl.BlockSpec((1, n, KV_LORA), lambda h: (h, 0, 0)),
        out_shape=jax.ShapeDtypeStruct((HEADS, n, KV_LORA), BF16),
        compiler_params=_params(1, 32),
        name="q_latent_sample",
    )(q_nope, w_nope)


def _head_proj_out_kernel(x_ref, w_ref, o_ref):
    o_ref[...] = _dot(x_ref[0], w_ref[0]).astype(BF16)


def _o_attn_sample(o_lat, w_v):
    n = o_lat.shape[1]
    return pl.pallas_call(
        _head_proj_out_kernel,
        grid=(HEADS,),
        in_specs=[pl.BlockSpec((1, n, KV_LORA), lambda h: (h, 0, 0)),
                  pl.BlockSpec((1, KV_LORA, V_DIM), lambda h: (h, 0, 0))],
        out_specs=pl.BlockSpec((n, V_DIM), lambda h: (0, h)),
        out_shape=jax.ShapeDtypeStruct((n, HEADS * V_DIM), BF16),
        compiler_params=_params(1, 32),
        name="o_attn_sample",
    )(o_lat, w_v)


DECODE_SLOTS = 4


def _decode_attn_kernel(pt_ref, qlat_ref, qr_ref, cnew_ref, krnew_ref, cache_c, cache_krt, o_ref,
                        cbuf, krbuf, sems, *, layer, n_seq, n_pages, group):
    n = pl.program_id(0)
    n_chunks = n_pages // group
    keys = group * PAGE
    ahead = DECODE_SLOTS - 1

    def page_copies(seq, chunk, g):
        slot = chunk % DECODE_SLOTS
        page = pt_ref[seq * n_pages + chunk * group + g]
        return (pltpu.make_async_copy(cache_c.at[layer, page], cbuf.at[slot, g], sems.at[0, slot]),
                pltpu.make_async_copy(cache_krt.at[layer, page], krbuf.at[slot, :, pl.ds(g * PAGE, PAGE)],
                                      sems.at[1, slot]))

    def start(seq, chunk):
        for g in range(group):
            for cp in page_copies(seq, chunk, g):
                cp.start()

    def wait(seq, chunk):
        for g in range(group):
            for cp in page_copies(seq, chunk, g):
                cp.wait()

    @pl.when(n == 0)
    def _():
        for ch in range(ahead):
            start(n, ch)

    q = qlat_ref[0]
    qr = qr_ref[0]
    m = jnp.full((HEADS, 1), -jnp.inf, F32)
    l = jnp.zeros((HEADS, 1), F32)
    acc = jnp.zeros((HEADS, KV_LORA), F32)
    for ch in range(n_chunks):
        nxt = ch + ahead
        if nxt < n_chunks:
            start(n, nxt)
        else:
            @pl.when(n + 1 < n_seq)
            def _():
                start(n + 1, nxt - n_chunks)
        wait(n, ch)
        slot = ch % DECODE_SLOTS
        cb = cbuf[slot].reshape(keys, KV_LORA).astype(BF16)
        krt = krbuf[slot].astype(BF16)
        s = (_dot_nt(q, cb) + _dot(qr, krt)) * SOFTMAX_LOG2_SCALE
        m_new = jnp.maximum(m, jnp.max(s, axis=-1, keepdims=True))
        alpha = jnp.exp2(m - m_new)
        p = jnp.exp2(s - m_new)
        l = alpha * l + jnp.sum(p, axis=-1, keepdims=True)
        acc = alpha * acc + _dot(p.astype(BF16), cb)
        m = m_new
    c_new = cnew_ref[0].astype(F32)
    s_new = (jnp.sum(q.astype(F32) * c_new, axis=-1, keepdims=True)
             + jnp.sum(qr.astype(F32) * krnew_ref[0].astype(F32), axis=-1, keepdims=True)) * SOFTMAX_LOG2_SCALE
    m_new = jnp.maximum(m, s_new)
    alpha = jnp.exp2(m - m_new)
    p_new = jnp.exp2(s_new - m_new)
    l = alpha * l + p_new
    acc = alpha * acc + p_new.astype(BF16).astype(F32) * c_new
    o_ref[0] = (acc / l).astype(BF16)


def _decode_attn(page_table, q_lat, q_rope, c_new, kr_new, cache_c, cache_krt, layer):
    n_seq, n_pages = page_table.shape
    group = 16
    while n_pages % (group * DECODE_SLOTS):
        group //= 2
    assert group >= 1, "pages per sequence must be a multiple of the ring depth"
    grid_spec = pltpu.PrefetchScalarGridSpec(
        num_scalar_prefetch=1,
        grid=(n_seq,),
        in_specs=[pl.BlockSpec((1, HEADS, KV_LORA), lambda n, pt: (n, 0, 0)),
                  pl.BlockSpec((1, HEADS, QK_ROPE), lambda n, pt: (n, 0, 0)),
                  pl.BlockSpec((1, 1, KV_LORA), lambda n, pt: (n, 0, 0)),
                  pl.BlockSpec((1, 1, QK_ROPE), lambda n, pt: (n, 0, 0)),
                  pl.BlockSpec(memory_space=pl.ANY),
                  pl.BlockSpec(memory_space=pl.ANY)],
        out_specs=pl.BlockSpec((1, HEADS, KV_LORA), lambda n, pt: (n, 0, 0)),
        scratch_shapes=[pltpu.VMEM((DECODE_SLOTS, group, PAGE, KV_LORA), F32),
                        pltpu.VMEM((DECODE_SLOTS, QK_ROPE, group * PAGE), F32),
                        pltpu.SemaphoreType.DMA((2, DECODE_SLOTS))],
    )
    return pl.pallas_call(
        functools.partial(_decode_attn_kernel, layer=layer, n_seq=n_seq, n_pages=n_pages, group=group),
        grid_spec=grid_spec,
        out_shape=jax.ShapeDtypeStruct((n_seq, HEADS, KV_LORA), BF16),
        compiler_params=_params(1, 48),
        name="decode_attn",
    )(page_table.reshape(-1), q_lat, q_rope, c_new, kr_new, cache_c, cache_krt)


def _ssm_conv_kernel(x_ref, halo_ref, w_ref, b_ref, o_ref, xp_s):
    tm = x_ref.shape[0]
    halo = halo_ref[...]
    xp_s[0:SUBLANES, :] = jnp.where(pl.program_id(0) == 0, jnp.zeros_like(halo), halo)
    xp_s[SUBLANES:, :] = x_ref[...]
    w = w_ref[...]
    base = SUBLANES - (SSM_CONV - 1)
    y = xp_s[base:base + tm, :] * w[0:1]
    for k in range(1, SSM_CONV):
        y = y + xp_s[base + k:base + k + tm, :] * w[k:k + 1]
    o_ref[...] = _silu(y + b_ref[...])


def _ssm_conv(proj, w, b):
    t = proj.shape[0]
    tm, tn = min(t, 512), 512
    c0 = P_XBC // tn
    return pl.pallas_call(
        _ssm_conv_kernel,
        grid=(t // tm, CONV_DIM // tn),
        in_specs=[pl.BlockSpec((tm, tn), lambda i, j: (i, c0 + j)),
                  pl.BlockSpec((SUBLANES, tn), lambda i, j: (jnp.maximum(i * (tm // SUBLANES) - 1, 0), c0 + j)),
                  pl.BlockSpec((SSM_CONV, tn), lambda i, j: (0, j)),
                  pl.BlockSpec((1, tn), lambda i, j: (0, j))],
        out_specs=pl.BlockSpec((tm, tn), lambda i, j: (i, j)),
        out_shape=jax.ShapeDtypeStruct((t, CONV_DIM), F32),
        scratch_shapes=[pltpu.VMEM((tm + SUBLANES, tn), F32)],
        compiler_params=_params(2, 32),
        name="ssm_conv",
    )(proj, proj, w, b)


def _ssm_conv_step_kernel(st_ref, x_ref, w_ref, b_ref, o_ref):
    w = w_ref[...]
    y = st_ref[0] * w[0:1]
    for k in range(1, SSM_CONV - 1):
        y = y + st_ref[k] * w[k:k + 1]
    y = y + x_ref[...] * w[SSM_CONV - 1:SSM_CONV]
    o_ref[...] = _silu(y + b_ref[...])


def _ssm_conv_step(state_t, proj, w, b):
    n = proj.shape[0]
    tn = 512
    c0 = P_XBC // tn
    return pl.pallas_call(
        _ssm_conv_step_kernel,
        grid=(CONV_DIM // tn,),
        in_specs=[pl.BlockSpec((SSM_CONV - 1, n, tn), lambda j: (0, 0, j)),
                  pl.BlockSpec((n, tn), lambda j: (0, c0 + j)),
                  pl.BlockSpec((SSM_CONV, tn), lambda j: (0, j)),
                  pl.BlockSpec((1, tn), lambda j: (0, j))],
        out_specs=pl.BlockSpec((n, tn), lambda j: (0, j)),
        out_shape=jax.ShapeDtypeStruct((n, CONV_DIM), F32),
        compiler_params=_params(1, 32),
        name="ssm_conv_step",
    )(state_t, proj, w, b)


def _ssd_chunk_kernel(xbc_ref, sm_ref, z_ref, dtb_ref, a_ref, dsk_ref, nw_ref, o_ref, hout_ref, st_s):
    c = pl.program_id(0)

    @pl.when(c == 0)
    def _():
        st_s[...] = jnp.zeros(st_s.shape, F32)

    ln = CHUNK
    dt = _softplus(sm_ref[:, LANES:2 * LANES] + dtb_ref[...])
    row = lax.broadcasted_iota(jnp.int32, (ln, ln), 0)
    col = lax.broadcasted_iota(jnp.int32, (ln, ln), 1)
    tri = row >= col
    cs = _dot_exact(tri.astype(F32), dt * a_ref[...])
    cs_t = cs.T
    dt_t = dt.T
    dtw_t = dt_t * jnp.exp(cs_t[:, ln - 1:ln] - cs_t)
    e_last = jnp.exp(cs[ln - 1:ln, :])
    lo_lane = lax.broadcasted_iota(jnp.int32, (ln, LANES), 1) < SSM_P
    lo_lane_row = lax.broadcasted_iota(jnp.int32, (1, LANES), 1) < SSM_P
    gw = D_INNER // SSM_GROUPS

    for g in range(SSM_GROUPS):
        b_f = xbc_ref[:, D_INNER + g * SSM_N:D_INNER + (g + 1) * SSM_N]
        c_f = xbc_ref[:, D_INNER + GN + g * SSM_N:D_INNER + GN + (g + 1) * SSM_N]
        cbt = _dot_nt(c_f.astype(BF16), b_f.astype(BF16))
        b_t = b_f.T
        gated = []
        for kk in range(PAIRS_PER_GROUP):
            k = g * PAIRS_PER_GROUP + kk
            sl = slice(k * LANES, (k + 1) * LANES)
            x2 = xbc_ref[:, sl]
            x2b = x2.astype(BF16)
            s_prev = st_s[k]
            rhs = jnp.concatenate([x2b, s_prev.astype(BF16)], axis=0)
            ys, sts = [], []
            for h in (2 * k, 2 * k + 1):
                cs_col = jnp.broadcast_to(cs[:, h:h + 1], (ln, ln))
                decay = jnp.exp(jnp.where(tri, cs_col - cs_t[h:h + 1, :], -jnp.inf))
                within = (cbt * decay * dt_t[h:h + 1, :]).astype(BF16)
                carried = (c_f * jnp.exp(cs_col)).astype(BF16)
                ys.append(_dot(jnp.concatenate([within, carried], axis=1), rhs))
                sts.append(_dot((b_t * dtw_t[h:h + 1, :]).astype(BF16), x2b))
            y = jnp.where(lo_lane, ys[0], ys[1]) + dsk_ref[:, sl] * x2
            keep = jnp.where(lo_lane_row, e_last[:, 2 * k:2 * k + 1], e_last[:, 2 * k + 1:2 * k + 2])
            st_s[k] = s_prev * keep + jnp.where(lo_lane, sts[0], sts[1])
            gated.append(y * _silu(z_ref[:, sl]))
        ssq = sum(jnp.sum(v * v, axis=-1, keepdims=True) for v in gated)
        scale = lax.rsqrt(ssq / gw + EPS)
        for kk, v in enumerate(gated):
            sl = slice((g * PAIRS_PER_GROUP + kk) * LANES, (g * PAIRS_PER_GROUP + kk + 1) * LANES)
            o_ref[:, sl] = (v * scale * nw_ref[:, sl]).astype(BF16)

    @pl.when(c == pl.num_programs(0) - 1)
    def _():
        for k in range(PAIRS):
            hout_ref[k] = st_s[k].T


def _ssd_chunked(xbc_act, proj, dt_bias_pad, a_pad, d_skip_cols, norm_w):
    t = xbc_act.shape[0]
    return pl.pallas_call(
        _ssd_chunk_kernel,
        grid=(t // CHUNK,),
        in_specs=[pl.BlockSpec((CHUNK, CONV_DIM), lambda c: (c, 0)),
                  pl.BlockSpec((CHUNK, SMALL_W), lambda c: (c, P_SMALL // SMALL_W)),
                  pl.BlockSpec((CHUNK, D_INNER), lambda c: (c, P_Z // D_INNER)),
                  pl.BlockSpec((1, LANES), lambda c: (0, 0)),
                  pl.BlockSpec((1, LANES), lambda c: (0, 0)),
                  pl.BlockSpec((1, D_INNER), lambda c: (0, 0)),
                  pl.BlockSpec((1, D_INNER), lambda c: (0, 0))],
        out_specs=[pl.BlockSpec((CHUNK, D_INNER), lambda c: (c, 0)),
                   pl.BlockSpec((PAIRS, LANES, SSM_N), lambda c: (0, 0, 0))],
        out_shape=[jax.ShapeDtypeStruct((t, D_INNER), BF16),
                   jax.ShapeDtypeStruct((PAIRS, LANES, SSM_N), F32)],
        scratch_shapes=[pltpu.VMEM((PAIRS, SSM_N, LANES), F32)],
        compiler_params=_params(1, 32),
        name="ssd_chunked",
    )(xbc_act, proj, proj, dt_bias_pad, a_pad, d_skip_cols, norm_w)


def _dt_prep_kernel(sm_ref, dtb_ref, a_ref, dt_ref, decay_ref):
    dt = _softplus(sm_ref[:, LANES:2 * LANES] + dtb_ref[...])
    dt_ref[...] = dt
    decay_ref[...] = jnp.exp(dt * a_ref[...])


def _dt_prep(proj, dt_bias_pad, a_pad):
    n = proj.shape[0]
    return pl.pallas_call(
        _dt_prep_kernel,
        grid=(1,),
        in_specs=[pl.BlockSpec((n, SMALL_W), lambda i: (0, P_SMALL // SMALL_W)),
                  pl.BlockSpec((1, LANES), lambda i: (0, 0)),
                  pl.BlockSpec((1, LANES), lambda i: (0, 0))],
        out_specs=[pl.BlockSpec((n, LANES), lambda i: (0, 0)), pl.BlockSpec((n, LANES), lambda i: (0, 0))],
        out_shape=[jax.ShapeDtypeStruct((n, LANES), F32), jax.ShapeDtypeStruct((n, LANES), F32)],
        compiler_params=_params(1, 32),
        name="dt_prep",
    )(proj, dt_bias_pad, a_pad)


def _ssd_step_kernel(decay_ref, st_ref, x_ref, b_ref, c_ref, dt_ref, dsk_ref, y_ref, so_ref):
    k = pl.program_id(0)
    n = x_ref.shape[0]
    head = lax.broadcasted_iota(jnp.int32, (LANES, LANES), 0)
    lane = lax.broadcasted_iota(jnp.int32, (LANES, LANES), 1)
    pick = (head == 2 * k + (lane >= SSM_P).astype(jnp.int32)).astype(F32)
    dt2 = _dot_exact(dt_ref[...], pick)
    x2 = x_ref[...]
    xd_t = (x2 * dt2).T
    c_t = c_ref[...].T.astype(BF16)
    ycols = jnp.zeros((LANES, n), F32)
    lane_n = lax.broadcasted_iota(jnp.int32, (LANES, n), 1)
    for t in range(n):
        upd = xd_t[:, t:t + 1] * b_ref[t:t + 1, :]
        new = []
        for hh in range(2):
            decay = decay_ref[t * SSM_HEADS + 2 * k + hh]
            new.append(st_ref[t, hh] * decay + upd[hh * SSM_P:(hh + 1) * SSM_P, :])
            so_ref[t, hh] = new[hh]
        y_all = _dot(jnp.concatenate(new, axis=0).astype(BF16), c_t)
        ycols = jnp.where(lane_n == t, y_all, ycols)
    y_ref[...] = ycols.T + dsk_ref[...] * x2


def _ssd_step(state, xbc_act, dt, decay, d_skip_cols):
    n = xbc_act.shape[0]
    assert n == LANES, "the per-sample SSD step transposes one (128, 128) tile of tokens"
    b0 = D_INNER // SSM_N
    c0 = (D_INNER + GN) // SSM_N
    st_spec = pl.BlockSpec((n, 2, SSM_P, SSM_N), lambda k: (0, k, 0, 0))
    return pl.pallas_call(
        _ssd_step_kernel,
        grid=(PAIRS,),
        in_specs=[pl.BlockSpec(memory_space=pltpu.SMEM),
                  st_spec,
                  pl.BlockSpec((n, LANES), lambda k: (0, k)),
                  pl.BlockSpec((n, SSM_N), lambda k: (0, b0 + k // PAIRS_PER_GROUP)),
                  pl.BlockSpec((n, SSM_N), lambda k: (0, c0 + k // PAIRS_PER_GROUP)),
                  pl.BlockSpec((n, LANES), lambda k: (0, 0)),
                  pl.BlockSpec((1, LANES), lambda k: (0, k))],
        out_specs=[pl.BlockSpec((n, LANES), lambda k: (0, k)), st_spec],
        out_shape=[jax.ShapeDtypeStruct((n, D_INNER), F32),
                   jax.ShapeDtypeStruct(state.shape, F32)],
        compiler_params=_params(1, 48),
        name="ssd_step",
    )(decay[:, :SSM_HEADS].reshape(-1), state, xbc_act, xbc_act, xbc_act, dt, d_skip_cols)


def _gate_norm_kernel(y_ref, z_ref, w_ref, o_ref):
    gw = D_INNER // SSM_GROUPS
    for g in range(SSM_GROUPS):
        sl = slice(g * gw, (g + 1) * gw)
        v = y_ref[:, sl] * _silu(z_ref[:, sl])
        o_ref[:, sl] = _rms(v, w_ref[:, sl]).astype(BF16)


def _gate_norm(y, proj, w):
    t = y.shape[0]
    tm = min(t, 512)
    return pl.pallas_call(
        _gate_norm_kernel,
        grid=(t // tm,),
        in_specs=[pl.BlockSpec((tm, D_INNER), lambda i: (i, 0)),
                  pl.BlockSpec((tm, D_INNER), lambda i: (i, P_Z // D_INNER)),
                  pl.BlockSpec((1, D_INNER), lambda i: (0, 0))],
        out_specs=pl.BlockSpec((tm, D_INNER), lambda i: (i, 0)),
        out_shape=jax.ShapeDtypeStruct((t, D_INNER), BF16),
        compiler_params=_params(1, 48),
        name="gate_norm",
    )(y, proj, w)


def _merge_kernel(oa_ref, os_ref, wa_ref, ws_ref, ga_ref, gs_ref, o_ref):
    a = _dot(oa_ref[...], wa_ref[...])
    s = _dot(os_ref[...], ws_ref[...])
    o_ref[...] = (jax.nn.sigmoid(ga_ref[...]) * a + jax.nn.sigmoid(gs_ref[...]) * s).astype(BF16)


def _merge(o_attn, o_ssm, w_attn, w_ssm, proj):
    t = o_attn.shape[0]
    tm, tn = min(t, 512), 512
    return pl.pallas_call(
        _merge_kernel,
        grid=(t // tm, D_MODEL // tn),
        in_specs=[pl.BlockSpec((tm, HEADS * V_DIM), lambda i, j: (i, 0)),
                  pl.BlockSpec((tm, D_INNER), lambda i, j: (i, 0)),
                  pl.BlockSpec((HEADS * V_DIM, tn), lambda i, j: (0, j)),
                  pl.BlockSpec((D_INNER, tn), lambda i, j: (0, j)),
                  pl.BlockSpec((tm, tn), lambda i, j: (i, P_GA // tn + j)),
                  pl.BlockSpec((tm, tn), lambda i, j: (i, P_GS // tn + j))],
        out_specs=pl.BlockSpec((tm, tn), lambda i, j: (i, j)),
        out_shape=jax.ShapeDtypeStruct((t, D_MODEL), BF16),
        compiler_params=_params(2, 48),
        name="merge",
    )(o_attn, o_ssm, w_attn, w_ssm, proj, proj)


def _out_proj_kernel(m_ref, w_ref, h_ref, npost_ref, npre_ref, h1_ref, hn_ref):
    h1 = h_ref[...] + _rms(_dot(m_ref[...], w_ref[...]), npost_ref[...])
    h1_ref[...] = h1
    hn_ref[...] = _rms(h1, npre_ref[...]).astype(BF16)


def _out_proj(merged, w_out, h, n_post, n_pre):
    t = h.shape[0]
    tm = min(t, 512)
    row = lambda i: (i, 0)
    fixed = lambda i: (0, 0)
    return pl.pallas_call(
        _out_proj_kernel,
        grid=(t // tm,),
        in_specs=[pl.BlockSpec((tm, D_MODEL), row), pl.BlockSpec((D_MODEL, D_MODEL), fixed),
                  pl.BlockSpec((tm, D_MODEL), row), pl.BlockSpec((1, D_MODEL), fixed),
                  pl.BlockSpec((1, D_MODEL), fixed)],
        out_specs=[pl.BlockSpec((tm, D_MODEL), row), pl.BlockSpec((tm, D_MODEL), row)],
        out_shape=[jax.ShapeDtypeStruct((t, D_MODEL), F32), jax.ShapeDtypeStruct((t, D_MODEL), BF16)],
        compiler_params=_params(1, 56),
        name="out_proj",
    )(merged, w_out, h, n_post, n_pre)


FFN_HALO = BF16_SUBLANES
FFN_ROW_CHUNKS = 2


def _ffn_kernel(x_ref, halo_ref, wg_ref, wv_ref, cwg_ref, cwv_ref, cbg_ref, cbv_ref, wd_ref, h_ref, nw_ref,
                o_ref, tg_ref, tv_ref, xe_s, u_s, acc_s):
    i, j = pl.program_id(0), pl.program_id(1)
    tm, tf = x_ref.shape[0], wg_ref.shape[1]

    @pl.when(j == 0)
    def _():
        halo = halo_ref[...]
        xe_s[0:FFN_HALO, :] = jnp.where(i == 0, jnp.zeros_like(halo), halo)
        xe_s[FFN_HALO:, :] = x_ref[...]
        acc_s[...] = jnp.zeros(acc_s.shape, F32)

    base = FFN_HALO - (FFN_CONV - 1)
    rc = tm // FFN_ROW_CHUNKS
    for c in range(FFN_ROW_CHUNKS):
        r0 = 0 if c == 0 else FFN_HALO + c * rc
        r1 = FFN_HALO + (c + 1) * rc
        xe = xe_s[r0:r1, :]
        u_s[r0:r1, 0:tf] = _dot(xe, wg_ref[...])
        u_s[r0:r1, tf:] = _dot(xe, wv_ref[...])
        t0 = base + c * rc

        def conv(lo, w_ref, b_ref):
            w = w_ref[...]
            y = u_s[t0:t0 + rc, lo:lo + tf] * w[0:1]
            for k in range(1, FFN_CONV):
                y = y + u_s[t0 + k:t0 + k + rc, lo:lo + tf] * w[k:k + 1]
            return y + b_ref[...]

        f = (_gelu_tanh(conv(0, cwg_ref, cbg_ref)) * conv(tf, cwv_ref, cbv_ref)).astype(BF16)
        acc_s[c * rc:(c + 1) * rc, :] += _dot(f, wd_ref[...])

    tail = FFN_HALO + tm - SUBLANES
    tg_ref[...] = u_s[tail:tail + SUBLANES, 0:tf]
    tv_ref[...] = u_s[tail:tail + SUBLANES, tf:]

    @pl.when(j == pl.num_programs(1) - 1)
    def _():
        o_ref[...] = h_ref[...] + _rms(acc_s[...], nw_ref[...])


def _ffn(hn, h, w_up, conv_w, conv_b, w_down, nw):
    t = h.shape[0]
    tm, tf = min(t, 512), 512
    nj = D_FF // tf
    gate = lambda i, j: (0, j)
    value = lambda i, j: (0, nj + j)
    row = lambda i, j: (i, 0)
    return pl.pallas_call(
        _ffn_kernel,
        grid=(t // tm, nj),
        in_specs=[pl.BlockSpec((tm, D_MODEL), row),
                  pl.BlockSpec((FFN_HALO, D_MODEL), lambda i, j: (jnp.maximum(i * (tm // FFN_HALO) - 1, 0), 0)),
                  pl.BlockSpec((D_MODEL, tf), gate), pl.BlockSpec((D_MODEL, tf), value),
                  pl.BlockSpec((FFN_CONV, tf), gate), pl.BlockSpec((FFN_CONV, tf), value),
                  pl.BlockSpec((1, tf), gate), pl.BlockSpec((1, tf), value),
                  pl.BlockSpec((tf, D_MODEL), lambda i, j: (j, 0)),
                  pl.BlockSpec((tm, D_MODEL), row),
                  pl.BlockSpec((1, D_MODEL), lambda i, j: (0, 0))],
        out_specs=[pl.BlockSpec((tm, D_MODEL), row),
                   pl.BlockSpec((SUBLANES, tf), lambda i, j: (i, j)),
                   pl.BlockSpec((SUBLANES, tf), lambda i, j: (i, j))],
        out_shape=[jax.ShapeDtypeStruct((t, D_MODEL), F32),
                   jax.ShapeDtypeStruct((t // tm * SUBLANES, D_FF), F32),
                   jax.ShapeDtypeStruct((t // tm * SUBLANES, D_FF), F32)],
        scratch_shapes=[pltpu.VMEM((tm + FFN_HALO, D_MODEL), BF16),
                        pltpu.VMEM((tm + FFN_HALO, 2 * tf), F32),
                        pltpu.VMEM((tm, D_MODEL), F32)],
        compiler_params=_params(2, 56),
        name="ffn",
    )(hn, hn, w_up, w_up, conv_w, conv_w, conv_b, conv_b, w_down, h, nw)


def _ffn_step_kernel(x_ref, wg_ref, wv_ref, sg_ref, sv_ref, cwg_ref, cwv_ref, cbg_ref, cbv_ref, wd_ref, h_ref,
                     nw_ref, o_ref, ug_ref, uv_ref, acc_s):
    j = pl.program_id(0)
    x = x_ref[...]

    def conv(u, s_ref, w_ref, b_ref):
        w = w_ref[...]
        y = s_ref[0] * w[0:1]
        for k in range(1, FFN_CONV - 1):
            y = y + s_ref[k] * w[k:k + 1]
        return y + u * w[FFN_CONV - 1:FFN_CONV] + b_ref[...]

    ug = _dot(x, wg_ref[...])
    uv = _dot(x, wv_ref[...])
    ug_ref[...] = ug
    uv_ref[...] = uv
    f = (_gelu_tanh(conv(ug, sg_ref, cwg_ref, cbg_ref)) * conv(uv, sv_ref, cwv_ref, cbv_ref)).astype(BF16)
    d = _dot(f, wd_ref[...])

    @pl.when(j == 0)
    def _():
        acc_s[...] = d

    @pl.when(j > 0)
    def _():
        acc_s[...] += d

    @pl.when(j == pl.num_programs(0) - 1)
    def _():
        o_ref[...] = h_ref[...] + _rms(acc_s[...], nw_ref[...])


def _ffn_step(hn, h, state_t, w_up, conv_w, conv_b, w_down, nw):
    n = h.shape[0]
    tf = 512
    nj = D_FF // tf
    gate = lambda j: (0, j)
    value = lambda j: (0, nj + j)
    fixed = lambda j: (0, 0)
    return pl.pallas_call(
        _ffn_step_kernel,
        grid=(nj,),
        in_specs=[pl.BlockSpec((n, D_MODEL), fixed),
                  pl.BlockSpec((D_MODEL, tf), gate), pl.BlockSpec((D_MODEL, tf), value),
                  pl.BlockSpec((FFN_CONV - 1, n, tf), lambda j: (0, 0, j)),
                  pl.BlockSpec((FFN_CONV - 1, n, tf), lambda j: (0, 0, nj + j)),
                  pl.BlockSpec((FFN_CONV, tf), gate), pl.BlockSpec((FFN_CONV, tf), value),
                  pl.BlockSpec((1, tf), gate), pl.BlockSpec((1, tf), value),
                  pl.BlockSpec((tf, D_MODEL), lambda j: (j, 0)),
                  pl.BlockSpec((n, D_MODEL), fixed),
                  pl.BlockSpec((1, D_MODEL), fixed)],
        out_specs=[pl.BlockSpec((n, D_MODEL), fixed), pl.BlockSpec((n, tf), gate), pl.BlockSpec((n, tf), gate)],
        out_shape=[jax.ShapeDtypeStruct((n, D_MODEL), F32),
                   jax.ShapeDtypeStruct((n, D_FF), F32), jax.ShapeDtypeStruct((n, D_FF), F32)],
        scratch_shapes=[pltpu.VMEM((n, D_MODEL), F32)],
        compiler_params=_params(1, 32),
        name="ffn_step",
    )(hn, w_up, w_up, state_t, state_t, conv_w, conv_w, conv_b, conv_b, w_down, h, nw)


def _ple_kernel(h_ref, pe_ref, wg_ref, wp_ref, nw_ref, o_ref):
    h = h_ref[...]
    gate = jax.nn.sigmoid(_dot(h.astype(BF16), wg_ref[...]))
    o_ref[...] = h + _rms(gate * _dot(pe_ref[...].astype(BF16), wp_ref[...]), nw_ref[...])


def _ple(h, pe, w_gate, w_proj, nw):
    t = h.shape[0]
    tm = min(t, 512)
    row = lambda i: (i, 0)
    fixed = lambda i: (0, 0)
    return pl.pallas_call(
        _ple_kernel,
        grid=(t // tm,),
        in_specs=[pl.BlockSpec((tm, D_MODEL), row), pl.BlockSpec((tm, PLE_DIM), row),
                  pl.BlockSpec((D_MODEL, D_MODEL), fixed), pl.BlockSpec((PLE_DIM, D_MODEL), fixed),
                  pl.BlockSpec((1, D_MODEL), fixed)],
        out_specs=pl.BlockSpec((tm, D_MODEL), row),
        out_shape=jax.ShapeDtypeStruct((t, D_MODEL), F32),
        compiler_params=_params(1, 56),
        name="ple",
    )(h, pe, w_gate, w_proj, nw)


def _rot_half_cols(w):
    half = QK_ROPE // 2
    return jnp.concatenate([-w[..., half:], w[..., :half]], axis=-1)


def _prepare_weights(w):
    w_in_t = w['w_in'].T
    splits = np.cumsum([0, Q_LORA, KV_LORA, QK_ROPE, D_INNER, CONV_DIM, SSM_HEADS, D_MODEL, D_MODEL])
    q_c, kv_c, k_r, z, xbc, dt, g_a, g_s = [w_in_t[a:b] for a, b in zip(splits[:-1], splits[1:])]
    pad = jnp.zeros((SMALL_W - 2 * QK_ROPE - SSM_HEADS, D_MODEL), F32)
    k_r_rot = _rot_half_cols(k_r.T).T
    packed_t = jnp.concatenate([z, g_a, g_s, xbc, kv_c, k_r, k_r_rot, dt, pad, q_c], axis=0)
    wq = w['w_q_up'].reshape(Q_LORA, HEADS, QK_NOPE + QK_ROPE)
    wq_rope = wq[:, :, QK_NOPE:]
    wkv = w['w_kv_up'].reshape(KV_LORA, HEADS, QK_NOPE + V_DIM)
    lane_pad = LANES - SSM_HEADS
    return dict(
        w_in=packed_t.astype(BF16),
        wq_nope=wq[:, :, :QK_NOPE].reshape(Q_LORA, HEADS * QK_NOPE).astype(BF16),
        wq_rope=wq_rope.reshape(Q_LORA, HEADS * QK_ROPE).astype(BF16),
        wq_rope_rot=_rot_half_cols(wq_rope).reshape(Q_LORA, HEADS * QK_ROPE).astype(BF16),
        w_nope=wkv[:, :, :QK_NOPE].transpose(1, 2, 0).astype(BF16),
        w_v=wkv[:, :, QK_NOPE:].transpose(1, 0, 2).astype(BF16),
        w_branch_attn=w['w_branch_attn'].astype(BF16),
        w_branch_ssm=w['w_branch_ssm'].astype(BF16),
        w_out=w['w_out'].astype(BF16),
        w_ffn_up=w['w_ffn_up'].astype(BF16),
        w_ffn_down=w['w_ffn_down'].astype(BF16),
        w_ple_gate=w['w_ple_gate'].astype(BF16),
        w_ple_proj=w['w_ple_proj'].astype(BF16),
        dt_bias=jnp.pad(w['dt_bias'], (0, lane_pad))[None, :],
        a_neg=jnp.pad(-jnp.exp(w['a_log']), (0, lane_pad))[None, :],
        d_skip=jnp.repeat(w['d_skip'], SSM_P)[None, :],
    )


def _rope_tables(pos):
    half = QK_ROPE // 2
    inv = ROPE_THETA ** (-jnp.arange(half, dtype=F32) / half)
    ang = pos[:, None] * inv[None, :]
    cos, sin = jnp.cos(ang), jnp.sin(ang)
    cos64, sin64 = jnp.concatenate([cos, cos], axis=1), jnp.concatenate([sin, sin], axis=1)
    return cos64, sin64, jnp.concatenate([cos64, cos64], axis=1), jnp.concatenate([sin64, sin64], axis=1)


def _vec(v):
    return v[None, :]


def _mixer_front(h, pos, w, pw):
    proj = _in_proj(h, _vec(w['norm_pre_mix']), pw['w_in'])
    cos64, sin64, cos128, sin128 = _rope_tables(pos)
    c, c_bf, kr, kr_bf = _kv_prep(proj, cos64, sin64, _vec(w['norm_kv']))
    q_nope, q_rope = _q_prep(proj, _vec(w['norm_q']), pw['wq_nope'], pw['wq_rope'], pw['wq_rope_rot'],
                             cos128, sin128)
    return proj, c, c_bf, kr, kr_bf, q_nope, q_rope


def _mixer_back(h, proj, o_attn, o_ssm, w, pw):
    merged = _merge(o_attn, o_ssm, pw['w_branch_attn'], pw['w_branch_ssm'], proj)
    return _out_proj(merged, pw['w_out'], h, _vec(w['norm_post_mix']), _vec(w['norm_pre_ffn']))


def _prompt_layer(h, pe, w, pw):
    t = h.shape[0]
    pos = jnp.arange(t, dtype=F32)
    proj, c, c_bf, kr, kr_bf, q_nope, q_rope = _mixer_front(h, pos, w, pw)
    o_attn = _prompt_attn(q_nope, q_rope, pw['w_nope'], pw['w_v'], c_bf, kr_bf)
    xbc_act = _ssm_conv(proj, w['conv_ssm_w'], _vec(w['conv_ssm_b']))
    o_ssm, st = _ssd_chunked(xbc_act, proj, pw['dt_bias'], pw['a_neg'], pw['d_skip'], _vec(w['norm_ssm']))
    h1, hn1 = _mixer_back(h, proj, o_attn, o_ssm, w, pw)
    h2, tail_g, tail_v = _ffn(hn1, h1, pw['w_ffn_up'], w['conv_ffn_w'], _vec(w['conv_ffn_b']),
                              pw['w_ffn_down'], _vec(w['norm_post_ffn']))
    h3 = _ple(h2, pe, pw['w_ple_gate'], pw['w_ple_proj'], _vec(w['norm_ple']))
    ssm_conv_new = proj[t - (SSM_CONV - 1):, P_XBC:P_XBC + CONV_DIM]
    last = tail_g.shape[0] - (FFN_CONV - 1)
    ffn_conv_new = jnp.concatenate([tail_g[last:], tail_v[last:]], axis=1)
    return h3, (c, kr, st.reshape(SSM_HEADS, SSM_P, SSM_N), ssm_conv_new, ffn_conv_new)


def _sample_layer(h, pe, past, w, pw, cache_c, cache_kr, layer, page_table, ssm_state, ssm_conv_state,
                  ffn_conv_state):
    n = h.shape[0]
    pos = jnp.full((n,), past, F32)
    proj, c, c_bf, kr, kr_bf, q_nope, q_rope = _mixer_front(h, pos, w, pw)
    q_lat = _q_latent_sample(q_nope, pw['w_nope']).transpose(1, 0, 2)
    o_lat = _decode_attn(page_table, q_lat, q_rope.reshape(n, HEADS, QK_ROPE), c_bf[:, None, :],
                         kr_bf[:, None, :], cache_c, jnp.swapaxes(cache_kr, 2, 3), layer)
    o_attn = _o_attn_sample(o_lat.transpose(1, 0, 2), pw['w_v'])
    xbc_act = _ssm_conv_step(ssm_conv_state.transpose(1, 0, 2), proj, w['conv_ssm_w'], _vec(w['conv_ssm_b']))
    dt, decay = _dt_prep(proj, pw['dt_bias'], pw['a_neg'])
    y_ssm, st = _ssd_step(ssm_state, xbc_act, dt, decay, pw['d_skip'])
    o_ssm = _gate_norm(y_ssm, proj, _vec(w['norm_ssm']))
    h1, hn1 = _mixer_back(h, proj, o_attn, o_ssm, w, pw)
    h2, u_g, u_v = _ffn_step(hn1, h1, ffn_conv_state.transpose(1, 0, 2), pw['w_ffn_up'], w['conv_ffn_w'],
                             _vec(w['conv_ffn_b']), pw['w_ffn_down'], _vec(w['norm_post_ffn']))
    h3 = _ple(h2, pe, pw['w_ple_gate'], pw['w_ple_proj'], _vec(w['norm_ple']))
    xbc_raw = proj[:, P_XBC:P_XBC + CONV_DIM]
    ssm_conv_new = jnp.concatenate([ssm_conv_state[:, 1:], xbc_raw[:, None, :]], axis=1)
    u = jnp.concatenate([u_g, u_v], axis=1)
    ffn_conv_new = jnp.concatenate([ffn_conv_state[:, 1:], u[:, None, :]], axis=1)
    return h3, (c[:, None, :], kr[:, None, :], st, ssm_conv_new, ffn_conv_new)


def kernel(x_prompt, x_sample, cache_kv_latent, cache_k_rope, state_ssm, state_ssm_conv, state_ffn_conv, page_table, p_prompt, p_sample, norm_pre_mix, w_in, norm_q, w_q_up, norm_kv, w_kv_up, w_branch_attn, conv_ssm_w, conv_ssm_b, dt_bias, a_log, d_skip, norm_ssm, w_branch_ssm, w_out, norm_post_mix, norm_pre_ffn, w_ffn_up, conv_ffn_w, conv_ffn_b, w_ffn_down, norm_post_ffn, w_ple_gate, w_ple_proj, norm_ple):
    assert x_prompt.shape[0] == 1, "one prompt sequence"
    assert x_sample.shape[1] == 1, "one new token per sample sequence"
    depth = w_in.shape[0]
    past = page_table.shape[1] * PAGE
    weights = dict(norm_pre_mix=norm_pre_mix, w_in=w_in, norm_q=norm_q, w_q_up=w_q_up, norm_kv=norm_kv,
                   w_kv_up=w_kv_up, w_branch_attn=w_branch_attn, conv_ssm_w=conv_ssm_w, conv_ssm_b=conv_ssm_b,
                   dt_bias=dt_bias, a_log=a_log, d_skip=d_skip, norm_ssm=norm_ssm, w_branch_ssm=w_branch_ssm,
                   w_out=w_out, norm_post_mix=norm_post_mix, norm_pre_ffn=norm_pre_ffn, w_ffn_up=w_ffn_up,
                   conv_ffn_w=conv_ffn_w, conv_ffn_b=conv_ffn_b, w_ffn_down=w_ffn_down,
                   norm_post_ffn=norm_post_ffn, w_ple_gate=w_ple_gate, w_ple_proj=w_ple_proj, norm_ple=norm_ple)
    hp, hs = x_prompt[0], x_sample[:, 0]
    outs_p, outs_s = [], []
    for d in range(depth):
        w = {k: v[d] for k, v in weights.items()}
        pw = _prepare_weights(w)
        hp, st_p = _prompt_layer(hp, p_prompt[d, 0], w, pw)
        hs, st_s = _sample_layer(hs, p_sample[d, :, 0], past, w, pw, cache_kv_latent, cache_k_rope, d,
                                 page_table, state_ssm[d], state_ssm_conv[d], state_ffn_conv[d])
        outs_p.append(st_p)
        outs_s.append(st_s)
    stack_p = [jnp.stack([o[k] for o in outs_p])[:, None] for k in range(5)]
    stack_s = [jnp.stack([o[k] for o in outs_s]) for k in range(5)]
    return (hp[None], hs[:, None], *stack_p, *stack_s)
```

```python
import functools

import jax
import jax.numpy as jnp
import numpy as np
from jax import lax
from jax.experimental import pallas as pl
from jax.experimental.pallas import tpu as pltpu

F32, BF16 = jnp.float32, jnp.bfloat16
HIGHEST = lax.Precision.HIGHEST

LANES = 128
SUBLANES = 8
BF16_SUBLANES = 16
V7X_VMEM_BYTES = 64 * 1024 * 1024

D_MODEL = 2048
PAGE = 128
HEADS = 16
QK_NOPE = 128
QK_ROPE = 64
V_DIM = 128
Q_LORA = 768
KV_LORA = 512
ROPE_THETA = 10000.0
ATTN_SCALE = (QK_NOPE + QK_ROPE) ** -0.5
SOFTMAX_LOG2_SCALE = ATTN_SCALE * float(np.log2(np.e))
SSM_HEADS = 32
SSM_P = 64
D_INNER = SSM_HEADS * SSM_P
SSM_GROUPS = 4
SSM_N = 128
SSM_CONV = 4
GN = SSM_GROUPS * SSM_N
CONV_DIM = D_INNER + 2 * GN
CHUNK = 128
PAIRS = SSM_HEADS // 2
PAIRS_PER_GROUP = PAIRS // SSM_GROUPS
D_FF = 4 * D_MODEL
FFN_CONV = 3
PLE_DIM = 256
EPS = 1e-6

P_Z = 0
P_GA = P_Z + D_INNER
P_GS = P_GA + D_MODEL
P_XBC = P_GS + D_MODEL
P_KV = P_XBC + CONV_DIM
P_SMALL = P_KV + KV_LORA
P_Q = P_SMALL + 2 * LANES
P_COLS = P_Q + Q_LORA
SMALL_W = 2 * LANES


def _params(n_axes, vmem_mib):
    return pltpu.CompilerParams(dimension_semantics=("arbitrary",) * n_axes,
                                vmem_limit_bytes=vmem_mib * 1024 * 1024)


def _rms(x, w):
    return x * lax.rsqrt(jnp.mean(x * x, axis=-1, keepdims=True) + EPS) * w


def _dot(a, b):
    return jnp.dot(a, b, preferred_element_type=F32)


def _dot_nt(a, b):
    return lax.dot_general(a, b, (((1,), (1,)), ((), ())), preferred_element_type=F32)


def _dot_exact(a, b):
    return jnp.dot(a, b, precision=HIGHEST, preferred_element_type=F32)


def _rep(x, n):
    return x if n == 1 else jnp.concatenate([x] * n, axis=1)


def _silu(x):
    return x * jax.nn.sigmoid(x)


def _softplus(x):
    return jnp.maximum(x, 0.0) + jnp.log1p(jnp.exp(-jnp.abs(x)))


def _gelu_tanh(x):
    return x * (0.5 * (1.0 + jnp.tanh(np.sqrt(2.0 / np.pi).astype(np.float32) * (x + 0.044715 * (x * x * x)))))


def _in_proj_kernel(x_ref, nw_ref, w_ref, o_ref, xn_ref):
    @pl.when(pl.program_id(1) == 0)
    def _():
        xn_ref[...] = _rms(x_ref[...], nw_ref[...]).astype(BF16)

    o_ref[...] = _dot_nt(xn_ref[...], w_ref[...])


def _in_proj(x, nw, w_packed_t):
    t = x.shape[0]
    tm, tn = min(t, 1024), 1536
    return pl.pallas_call(
        _in_proj_kernel,
        grid=(t // tm, P_COLS // tn),
        in_specs=[pl.BlockSpec((tm, D_MODEL), lambda i, j: (i, 0)),
                  pl.BlockSpec((1, D_MODEL), lambda i, j: (0, 0)),
                  pl.BlockSpec((tn, D_MODEL), lambda i, j: (j, 0))],
        out_specs=pl.BlockSpec((tm, tn), lambda i, j: (i, j)),
        out_shape=jax.ShapeDtypeStruct((t, P_COLS), F32),
        scratch_shapes=[pltpu.VMEM((tm, D_MODEL), BF16)],
        compiler_params=_params(2, 56),
        name="in_proj",
    )(x, nw, w_packed_t)


def _kv_prep_kernel(kv_ref, sm_ref, cos_ref, sin_ref, nw_ref, c_ref, cbf_ref, kr_ref, krbf_ref):
    c = _rms(kv_ref[...], nw_ref[...])
    c_ref[...] = c
    cbf_ref[...] = c.astype(BF16)
    sm = sm_ref[...]
    kr = sm[:, 0:QK_ROPE] * cos_ref[...] + sm[:, QK_ROPE:2 * QK_ROPE] * sin_ref[...]
    kr_ref[...] = kr
    krbf_ref[...] = kr.astype(BF16)


def _kv_prep(proj, cos64, sin64, nw):
    t = proj.shape[0]
    tm = min(t, 512)
    return pl.pallas_call(
        _kv_prep_kernel,
        grid=(t // tm,),
        in_specs=[pl.BlockSpec((tm, KV_LORA), lambda i: (i, P_KV // KV_LORA)),
                  pl.BlockSpec((tm, SMALL_W), lambda i: (i, P_SMALL // SMALL_W)),
                  pl.BlockSpec((tm, QK_ROPE), lambda i: (i, 0)),
                  pl.BlockSpec((tm, QK_ROPE), lambda i: (i, 0)),
                  pl.BlockSpec((1, KV_LORA), lambda i: (0, 0))],
        out_specs=[pl.BlockSpec((tm, KV_LORA), lambda i: (i, 0)),
                   pl.BlockSpec((tm, KV_LORA), lambda i: (i, 0)),
                   pl.BlockSpec((tm, QK_ROPE), lambda i: (i, 0)),
                   pl.BlockSpec((tm, QK_ROPE), lambda i: (i, 0))],
        out_shape=[jax.ShapeDtypeStruct((t, KV_LORA), F32), jax.ShapeDtypeStruct((t, KV_LORA), BF16),
                   jax.ShapeDtypeStruct((t, QK_ROPE), F32), jax.ShapeDtypeStruct((t, QK_ROPE), BF16)],
        compiler_params=_params(1, 32),
        name="kv_prep",
    )(proj, proj, cos64, sin64, nw)


def _q_prep_kernel(q_ref, nw_ref, wn_ref, wr_ref, wrr_ref, cos_ref, sin_ref, qn_ref, qr_ref):
    qn = _rms(q_ref[...], nw_ref[...]).astype(BF16)
    qn_ref[...] = _dot(qn, wn_ref[...]).astype(BF16)
    r = _dot(qn, wr_ref[...])
    rr = _dot(qn, wrr_ref[...])
    cos, sin = cos_ref[...], sin_ref[...]
    for k in range(HEADS * QK_ROPE // LANES):
        sl = slice(k * LANES, (k + 1) * LANES)
        qr_ref[:, sl] = (r[:, sl] * cos + rr[:, sl] * sin).astype(BF16)


def _q_prep(proj, nw, wq_nope, wq_rope, wq_rope_rot, cos128, sin128):
    t = proj.shape[0]
    tm = min(t, 512)
    n_nope, n_rope = HEADS * QK_NOPE, HEADS * QK_ROPE
    return pl.pallas_call(
        _q_prep_kernel,
        grid=(t // tm,),
        in_specs=[pl.BlockSpec((tm, Q_LORA), lambda i: (i, P_Q // Q_LORA)),
                  pl.BlockSpec((1, Q_LORA), lambda i: (0, 0)),
                  pl.BlockSpec((Q_LORA, n_nope), lambda i: (0, 0)),
                  pl.BlockSpec((Q_LORA, n_rope), lambda i: (0, 0)),
                  pl.BlockSpec((Q_LORA, n_rope), lambda i: (0, 0)),
                  pl.BlockSpec((tm, LANES), lambda i: (i, 0)),
                  pl.BlockSpec((tm, LANES), lambda i: (i, 0))],
        out_specs=[pl.BlockSpec((tm, n_nope), lambda i: (i, 0)),
                   pl.BlockSpec((tm, n_rope), lambda i: (i, 0))],
        out_shape=[jax.ShapeDtypeStruct((t, n_nope), BF16), jax.ShapeDtypeStruct((t, n_rope), BF16)],
        compiler_params=_params(1, 48),
        name="q_prep",
    )(proj, nw, wq_nope, wq_rope, wq_rope_rot, cos128, sin128)


def _prompt_attn_kernel(qi_ref, kj_ref, qn_ref, qr_ref, wn_ref, wv_ref, c_ref, kr_ref, o_ref,
                        qlat_s, qrope_s, m_s, l_s, acc_s, *, tq):
    step = pl.program_id(0)
    i = qi_ref[step]
    j = kj_ref[step]
    j_last = i // 4
    r = i % 4

    @pl.when(j == 0)
    def _init():
        for h in range(HEADS):
            rs = slice(h * tq, (h + 1) * tq)
            qlat_s[rs, :] = _dot(qn_ref[:, h * QK_NOPE:(h + 1) * QK_NOPE], wn_ref[h]).astype(BF16)
            qrope_s[rs, :] = qr_ref[:, h * QK_ROPE:(h + 1) * QK_ROPE]
        m_s[...] = jnp.full(m_s.shape, -jnp.inf, F32)
        l_s[...] = jnp.zeros(l_s.shape, F32)
        acc_s[...] = jnp.zeros(acc_s.shape, F32)

    def scores(k0, nk):
        return (_dot_nt(qlat_s[...], c_ref[pl.ds(k0, nk), :])
                + _dot_nt(qrope_s[...], kr_ref[pl.ds(k0, nk), :])) * SOFTMAX_LOG2_SCALE

    def update(s, k0, nk, limit=None):
        if limit is not None:
            visible = (lax.broadcasted_iota(jnp.int32, (1, tq, nk), 2)
                       <= lax.broadcasted_iota(jnp.int32, (1, tq, nk), 1) + limit)
            s = jnp.where(visible, s.reshape(HEADS, tq, nk), -jnp.inf).reshape(HEADS * tq, nk)
        m_prev = m_s[...]
        m_new = jnp.maximum(m_prev, jnp.max(s, axis=-1, keepdims=True))
        alpha = jnp.exp2(m_prev - m_new)
        p = jnp.exp2(s - _rep(m_new, nk // LANES))
        l_s[...] = alpha * l_s[...] + jnp.sum(p, axis=-1, keepdims=True)
        acc_s[...] = acc_s[...] * _rep(alpha, KV_LORA // LANES) + _dot(p.astype(BF16), c_ref[pl.ds(k0, nk), :])
        m_s[...] = m_new

    @pl.when(j < j_last)
    def _below_diagonal():
        s_a = scores(0, 2 * tq)
        s_b = scores(2 * tq, 2 * tq)
        update(s_a, 0, 2 * tq)
        update(s_b, 2 * tq, 2 * tq)

    @pl.when(j == j_last)
    def _diagonal_block():
        def quarter(q, carry):
            k0 = pl.multiple_of(q * tq, tq)
            update(scores(k0, tq), k0, tq, limit=(r - q) * tq)
            return carry

        lax.fori_loop(0, r + 1, quarter, 0)

    @pl.when(j == j_last)
    def _finish():
        for h in range(HEADS):
            rs = slice(h * tq, (h + 1) * tq)
            o_lat = (acc_s[rs, :] / _rep(l_s[rs, :], KV_LORA // LANES)).astype(BF16)
            o_ref[:, h * V_DIM:(h + 1) * V_DIM] = _dot(o_lat, wv_ref[h]).astype(BF16)


def _prompt_attn(q_nope, q_rope, w_nope, w_v, c_bf, kr_bf):
    t = q_nope.shape[0]
    tq = 256
    tk = 4 * tq
    assert t % tk == 0, "prompt length must be a multiple of the key block"
    pairs = [(i, j) for i in range(t // tq) for j in range(i // 4 + 1)]
    qi = jnp.asarray([p[0] for p in pairs], jnp.int32)
    kj = jnp.asarray([p[1] for p in pairs], jnp.int32)
    rows = HEADS * tq
    grid_spec = pltpu.PrefetchScalarGridSpec(
        num_scalar_prefetch=2,
        grid=(len(pairs),),
        in_specs=[pl.BlockSpec((tq, HEADS * QK_NOPE), lambda s, qi, kj: (qi[s], 0)),
                  pl.BlockSpec((tq, HEADS * QK_ROPE), lambda s, qi, kj: (qi[s], 0)),
                  pl.BlockSpec((HEADS, QK_NOPE, KV_LORA), lambda s, qi, kj: (0, 0, 0)),
                  pl.BlockSpec((HEADS, KV_LORA, V_DIM), lambda s, qi, kj: (0, 0, 0)),
                  pl.BlockSpec((tk, KV_LORA), lambda s, qi, kj: (kj[s], 0)),
                  pl.BlockSpec((tk, QK_ROPE), lambda s, qi, kj: (kj[s], 0))],
        out_specs=pl.BlockSpec((tq, HEADS * V_DIM), lambda s, qi, kj: (qi[s], 0)),
        scratch_shapes=[pltpu.VMEM((rows, KV_LORA), BF16), pltpu.VMEM((rows, QK_ROPE), BF16),
                        pltpu.VMEM((rows, LANES), F32), pltpu.VMEM((rows, LANES), F32),
                        pltpu.VMEM((rows, KV_LORA), F32)],
    )
    return pl.pallas_call(
        functools.partial(_prompt_attn_kernel, tq=tq),
        grid_spec=grid_spec,
        out_shape=jax.ShapeDtypeStruct((t, HEADS * V_DIM), BF16),
        compiler_params=_params(1, 56),
        name="prompt_attn",
    )(qi, kj, q_nope, q_rope, w_nope, w_v, c_bf, kr_bf)


def _head_proj_in_kernel(x_ref, w_ref, o_ref):
    o_ref[0] = _dot(x_ref[...], w_ref[0]).astype(BF16)


def _q_latent_sample(q_nope, w_nope):
    n = q_nope.shape[0]
    return pl.pallas_call(
        _head_proj_in_kernel,
        grid=(HEADS,),
        in_specs=[pl.BlockSpec((n, QK_NOPE), lambda h: (0, h)),
                  pl.BlockSpec((1, QK_NOPE, KV_LORA), lambda h: (h, 0, 0))],
        out_specs=pl.BlockSpec((1, n, KV_LORA), lambda h: (h, 0, 0)),
        out_shape=jax.ShapeDtypeStruct((HEADS, n, KV_LORA), BF16),
        compiler_params=_params(1, 32),
        name="q_latent_sample",
    )(q_nope, w_nope)


def _head_proj_out_kernel(x_ref, w_ref, o_ref):
    o_ref[...] = _dot(x_ref[0], w_ref[0]).astype(BF16)


def _o_attn_sample(o_lat, w_v):
    n = o_lat.shape[1]
    return pl.pallas_call(
        _head_proj_out_kernel,
        grid=(HEADS,),
        in_specs=[pl.BlockSpec((1, n, KV_LORA), lambda h: (h, 0, 0)),
                  pl.BlockSpec((1, KV_LORA, V_DIM), lambda h: (h, 0, 0))],
        out_specs=pl.BlockSpec((n, V_DIM), lambda h: (0, h)),
        out_shape=jax.ShapeDtypeStruct((n, HEADS * V_DIM), BF16),
        compiler_params=_params(1, 32),
        name="o_attn_sample",
    )(o_lat, w_v)


DECODE_SLOTS = 4


def _decode_attn_kernel(pt_ref, qlat_ref, qr_ref, cnew_ref, krnew_ref, cache_c, cache_krt, o_ref,
                        cbuf, krbuf, sems, *, layer, n_seq, n_pages, group):
    n = pl.program_id(0)
    n_chunks = n_pages // group
    keys = group * PAGE
    ahead = DECODE_SLOTS - 1

    def page_copies(seq, chunk, g):
        slot = chunk % DECODE_SLOTS
        page = pt_ref[seq * n_pages + chunk * group + g]
        return (pltpu.make_async_copy(cache_c.at[layer, page], cbuf.at[slot, g], sems.at[0, slot]),
                pltpu.make_async_copy(cache_krt.at[layer, page], krbuf.at[slot, :, pl.ds(g * PAGE, PAGE)],
                                      sems.at[1, slot]))

    def start(seq, chunk):
        for g in range(group):
            for cp in page_copies(seq, chunk, g):
                cp.start()

    def wait(seq, chunk):
        for g in range(group):
            for cp in page_copies(seq, chunk, g):
                cp.wait()

    @pl.when(n == 0)
    def _():
        for ch in range(ahead):
            start(n, ch)

    q = qlat_ref[0]
    qr = qr_ref[0]
    m = jnp.full((HEADS, 1), -jnp.inf, F32)
    l = jnp.zeros((HEADS, 1), F32)
    acc = jnp.zeros((HEADS, KV_LORA), F32)
    for ch in range(n_chunks):
        nxt = ch + ahead
        if nxt < n_chunks:
            start(n, nxt)
        else:
            @pl.when(n + 1 < n_seq)
            def _():
                start(n + 1, nxt - n_chunks)
        wait(n, ch)
        slot = ch % DECODE_SLOTS
        cb = cbuf[slot].reshape(keys, KV_LORA).astype(BF16)
        krt = krbuf[slot].astype(BF16)
        s = (_dot_nt(q, cb) + _dot(qr, krt)) * SOFTMAX_LOG2_SCALE
        m_new = jnp.maximum(m, jnp.max(s, axis=-1, keepdims=True))
        alpha = jnp.exp2(m - m_new)
        p = jnp.exp2(s - m_new)
        l = alpha * l + jnp.sum(p, axis=-1, keepdims=True)
        acc = alpha * acc + _dot(p.astype(BF16), cb)
        m = m_new
    c_new = cnew_ref[0].astype(F32)
    s_new = (jnp.sum(q.astype(F32) * c_new, axis=-1, keepdims=True)
             + jnp.sum(qr.astype(F32) * krnew_ref[0].astype(F32), axis=-1, keepdims=True)) * SOFTMAX_LOG2_SCALE
    m_new = jnp.maximum(m, s_new)
    alpha = jnp.exp2(m - m_new)
    p_new = jnp.exp2(s_new - m_new)
    l = alpha * l + p_new
    acc = alpha * acc + p_new.astype(BF16).astype(F32) * c_new
    o_ref[0] = (acc / l).astype(BF16)


def _decode_attn(page_table, q_lat, q_rope, c_new, kr_new, cache_c, cache_krt, layer):
    n_seq, n_pages = page_table.shape
    group = 16
    while n_pages % (group * DECODE_SLOTS):
        group //= 2
    assert group >= 1, "pages per sequence must be a multiple of the ring depth"
    grid_spec = pltpu.PrefetchScalarGridSpec(
        num_scalar_prefetch=1,
        grid=(n_seq,),
        in_specs=[pl.BlockSpec((1, HEADS, KV_LORA), lambda n, pt: (n, 0, 0)),
                  pl.BlockSpec((1, HEADS, QK_ROPE), lambda n, pt: (n, 0, 0)),
                  pl.BlockSpec((1, 1, KV_LORA), lambda n, pt: (n, 0, 0)),
                  pl.BlockSpec((1, 1, QK_ROPE), lambda n, pt: (n, 0, 0)),
                  pl.BlockSpec(memory_space=pl.ANY),
                  pl.BlockSpec(memory_space=pl.ANY)],
        out_specs=pl.BlockSpec((1, HEADS, KV_LORA), lambda n, pt: (n, 0, 0)),
        scratch_shapes=[pltpu.VMEM((DECODE_SLOTS, group, PAGE, KV_LORA), F32),
                        pltpu.VMEM((DECODE_SLOTS, QK_ROPE, group * PAGE), F32),
                        pltpu.SemaphoreType.DMA((2, DECODE_SLOTS))],
    )
    return pl.pallas_call(
        functools.partial(_decode_attn_kernel, layer=layer, n_seq=n_seq, n_pages=n_pages, group=group),
        grid_spec=grid_spec,
        out_shape=jax.ShapeDtypeStruct((n_seq, HEADS, KV_LORA), BF16),
        compiler_params=_params(1, 48),
        name="decode_attn",
    )(page_table.reshape(-1), q_lat, q_rope, c_new, kr_new, cache_c, cache_krt)


def _ssm_conv_kernel(x_ref, halo_ref, w_ref, b_ref, o_ref, xp_s):
    tm = x_ref.shape[0]
    halo = halo_ref[...]
    xp_s[0:SUBLANES, :] = jnp.where(pl.program_id(0) == 0, jnp.zeros_like(halo), halo)
    xp_s[SUBLANES:, :] = x_ref[...]
    w = w_ref[...]
    base = SUBLANES - (SSM_CONV - 1)
    y = xp_s[base:base + tm, :] * w[0:1]
    for k in range(1, SSM_CONV):
        y = y + xp_s[base + k:base + k + tm, :] * w[k:k + 1]
    o_ref[...] = _silu(y + b_ref[...])


def _ssm_conv(proj, w, b):
    t = proj.shape[0]
    tm, tn = min(t, 512), 512
    c0 = P_XBC // tn
    return pl.pallas_call(
        _ssm_conv_kernel,
        grid=(t // tm, CONV_DIM // tn),
        in_specs=[pl.BlockSpec((tm, tn), lambda i, j: (i, c0 + j)),
                  pl.BlockSpec((SUBLANES, tn), lambda i, j: (jnp.maximum(i * (tm // SUBLANES) - 1, 0), c0 + j)),
                  pl.BlockSpec((SSM_CONV, tn), lambda i, j: (0, j)),
                  pl.BlockSpec((1, tn), lambda i, j: (0, j))],
        out_specs=pl.BlockSpec((tm, tn), lambda i, j: (i, j)),
        out_shape=jax.ShapeDtypeStruct((t, CONV_DIM), F32),
        scratch_shapes=[pltpu.VMEM((tm + SUBLANES, tn), F32)],
        compiler_params=_params(2, 32),
        name="ssm_conv",
    )(proj, proj, w, b)


def _ssm_conv_step_kernel(st_ref, x_ref, w_ref, b_ref, o_ref):
    w = w_ref[...]
    y = st_ref[0] * w[0:1]
    for k in range(1, SSM_CONV - 1):
        y = y + st_ref[k] * w[k:k + 1]
    y = y + x_ref[...] * w[SSM_CONV - 1:SSM_CONV]
    o_ref[...] = _silu(y + b_ref[...])


def _ssm_conv_step(state_t, proj, w, b):
    n = proj.shape[0]
    tn = 512
    c0 = P_XBC // tn
    return pl.pallas_call(
        _ssm_conv_step_kernel,
        grid=(CONV_DIM // tn,),
        in_specs=[pl.BlockSpec((SSM_CONV - 1, n, tn), lambda j: (0, 0, j)),
                  pl.BlockSpec((n, tn), lambda j: (0, c0 + j)),
                  pl.BlockSpec((SSM_CONV, tn), lambda j: (0, j)),
                  pl.BlockSpec((1, tn), lambda j: (0, j))],
        out_specs=pl.BlockSpec((n, tn), lambda j: (0, j)),
        out_shape=jax.ShapeDtypeStruct((n, CONV_DIM), F32),
        compiler_params=_params(1, 32),
        name="ssm_conv_step",
    )(state_t, proj, w, b)


def _ssd_chunk_kernel(xbc_ref, sm_ref, z_ref, dtb_ref, a_ref, dsk_ref, nw_ref, o_ref, hout_ref, st_s):
    c = pl.program_id(0)

    @pl.when(c == 0)
    def _():
        st_s[...] = jnp.zeros(st_s.shape, F32)

    ln = CHUNK
    dt = _softplus(sm_ref[:, LANES:2 * LANES] + dtb_ref[...])
    row = lax.broadcasted_iota(jnp.int32, (ln, ln), 0)
    col = lax.broadcasted_iota(jnp.int32, (ln, ln), 1)
    tri = row >= col
    cs = _dot_exact(tri.astype(F32), dt * a_ref[...])
    cs_t = cs.T
    dt_t = dt.T
    dtw_t = dt_t * jnp.exp(cs_t[:, ln - 1:ln] - cs_t)
    e_last = jnp.exp(cs[ln - 1:ln, :])
    lo_lane = lax.broadcasted_iota(jnp.int32, (ln, LANES), 1) < SSM_P
    lo_lane_row = lax.broadcasted_iota(jnp.int32, (1, LANES), 1) < SSM_P
    gw = D_INNER // SSM_GROUPS

    for g in range(SSM_GROUPS):
        b_f = xbc_ref[:, D_INNER + g * SSM_N:D_INNER + (g + 1) * SSM_N]
        c_f = xbc_ref[:, D_INNER + GN + g * SSM_N:D_INNER + GN + (g + 1) * SSM_N]
        cbt = _dot_nt(c_f.astype(BF16), b_f.astype(BF16))
        b_t = b_f.T
        gated = []
        for kk in range(PAIRS_PER_GROUP):
            k = g * PAIRS_PER_GROUP + kk
            sl = slice(k * LANES, (k + 1) * LANES)
            x2 = xbc_ref[:, sl]
            x2b = x2.astype(BF16)
            s_prev = st_s[k]
            rhs = jnp.concatenate([x2b, s_prev.astype(BF16)], axis=0)
            ys, sts = [], []
            for h in (2 * k, 2 * k + 1):
                cs_col = jnp.broadcast_to(cs[:, h:h + 1], (ln, ln))
                decay = jnp.exp(jnp.where(tri, cs_col - cs_t[h:h + 1, :], -jnp.inf))
                within = (cbt * decay * dt_t[h:h + 1, :]).astype(BF16)
                carried = (c_f * jnp.exp(cs_col)).astype(BF16)
                ys.append(_dot(jnp.concatenate([within, carried], axis=1), rhs))
                sts.append(_dot((b_t * dtw_t[h:h + 1, :]).astype(BF16), x2b))
            y = jnp.where(lo_lane, ys[0], ys[1]) + dsk_ref[:, sl] * x2
            keep = jnp.where(lo_lane_row, e_last[:, 2 * k:2 * k + 1], e_last[:, 2 * k + 1:2 * k + 2])
            st_s[k] = s_prev * keep + jnp.where(lo_lane, sts[0], sts[1])
            gated.append(y * _silu(z_ref[:, sl]))
        ssq = sum(jnp.sum(v * v, axis=-1, keepdims=True) for v in gated)
        scale = lax.rsqrt(ssq / gw + EPS)
        for kk, v in enumerate(gated):
            sl = slice((g * PAIRS_PER_GROUP + kk) * LANES, (g * PAIRS_PER_GROUP + kk + 1) * LANES)
            o_ref[:, sl] = (v * scale * nw_ref[:, sl]).astype(BF16)

    @pl.when(c == pl.num_programs(0) - 1)
    def _():
        for k in range(PAIRS):
            hout_ref[k] = st_s[k].T


def _ssd_chunked(xbc_act, proj, dt_bias_pad, a_pad, d_skip_cols, norm_w):
    t = xbc_act.shape[0]
    return pl.pallas_call(
        _ssd_chunk_kernel,
        grid=(t // CHUNK,),
        in_specs=[pl.BlockSpec((CHUNK, CONV_DIM), lambda c: (c, 0)),
                  pl.BlockSpec((CHUNK, SMALL_W), lambda c: (c, P_SMALL // SMALL_W)),
                  pl.BlockSpec((CHUNK, D_INNER), lambda c: (c, P_Z // D_INNER)),
                  pl.BlockSpec((1, LANES), lambda c: (0, 0)),
                  pl.BlockSpec((1, LANES), lambda c: (0, 0)),
                  pl.BlockSpec((1, D_INNER), lambda c: (0, 0)),
                  pl.BlockSpec((1, D_INNER), lambda c: (0, 0))],
        out_specs=[pl.BlockSpec((CHUNK, D_INNER), lambda c: (c, 0)),
                   pl.BlockSpec((PAIRS, LANES, SSM_N), lambda c: (0, 0, 0))],
        out_shape=[jax.ShapeDtypeStruct((t, D_INNER), BF16),
                   jax.ShapeDtypeStruct((PAIRS, LANES, SSM_N), F32)],
        scratch_shapes=[pltpu.VMEM((PAIRS, SSM_N, LANES), F32)],
        compiler_params=_params(1, 32),
        name="ssd_chunked",
    )(xbc_act, proj, proj, dt_bias_pad, a_pad, d_skip_cols, norm_w)


def _dt_prep_kernel(sm_ref, dtb_ref, a_ref, dt_ref, decay_ref):
    dt = _softplus(sm_ref[:, LANES:2 * LANES] + dtb_ref[...])
    dt_ref[...] = dt
    decay_ref[...] = jnp.exp(dt * a_ref[...])


def _dt_prep(proj, dt_bias_pad, a_pad):
    n = proj.shape[0]
    return pl.pallas_call(
        _dt_prep_kernel,
        grid=(1,),
        in_specs=[pl.BlockSpec((n, SMALL_W), lambda i: (0, P_SMALL // SMALL_W)),
                  pl.BlockSpec((1, LANES), lambda i: (0, 0)),
                  pl.BlockSpec((1, LANES), lambda i: (0, 0))],
        out_specs=[pl.BlockSpec((n, LANES), lambda i: (0, 0)), pl.BlockSpec((n, LANES), lambda i: (0, 0))],
        out_shape=[jax.ShapeDtypeStruct((n, LANES), F32), jax.ShapeDtypeStruct((n, LANES), F32)],
        compiler_params=_params(1, 32),
        name="dt_prep",
    )(proj, dt_bias_pad, a_pad)


def _ssd_step_kernel(decay_ref, st_ref, x_ref, b_ref, c_ref, dt_ref, dsk_ref, y_ref, so_ref):
    k = pl.program_id(0)
    n = x_ref.shape[0]
    head = lax.broadcasted_iota(jnp.int32, (LANES, LANES), 0)
    lane = lax.broadcasted_iota(jnp.int32, (LANES, LANES), 1)
    pick = (head == 2 * k + (lane >= SSM_P).astype(jnp.int32)).astype(F32)
    dt2 = _dot_exact(dt_ref[...], pick)
    x2 = x_ref[...]
    xd_t = (x2 * dt2).T
    c_t = c_ref[...].T.astype(BF16)
    ycols = jnp.zeros((LANES, n), F32)
    lane_n = lax.broadcasted_iota(jnp.int32, (LANES, n), 1)
    for t in range(n):
        upd = xd_t[:, t:t + 1] * b_ref[t:t + 1, :]
        new = []
        for hh in range(2):
            decay = decay_ref[t * SSM_HEADS + 2 * k + hh]
            new.append(st_ref[t, hh] * decay + upd[hh * SSM_P:(hh + 1) * SSM_P, :])
            so_ref[t, hh] = new[hh]
        y_all = _dot(jnp.concatenate(new, axis=0).astype(BF16), c_t)
        ycols = jnp.where(lane_n == t, y_all, ycols)
    y_ref[...] = ycols.T + dsk_ref[...] * x2


def _ssd_step(state, xbc_act, dt, decay, d_skip_cols):
    n = xbc_act.shape[0]
    assert n == LANES, "the per-sample SSD step transposes one (128, 128) tile of tokens"
    b0 = D_INNER // SSM_N
    c0 = (D_INNER + GN) // SSM_N
    st_spec = pl.BlockSpec((n, 2, SSM_P, SSM_N), lambda k: (0, k, 0, 0))
    return pl.pallas_call(
        _ssd_step_kernel,
        grid=(PAIRS,),
        in_specs=[pl.BlockSpec(memory_space=pltpu.SMEM),
                  st_spec,
                  pl.BlockSpec((n, LANES), lambda k: (0, k)),
                  pl.BlockSpec((n, SSM_N), lambda k: (0, b0 + k // PAIRS_PER_GROUP)),
                  pl.BlockSpec((n, SSM_N), lambda k: (0, c0 + k // PAIRS_PER_GROUP)),
                  pl.BlockSpec((n, LANES), lambda k: (0, 0)),
                  pl.BlockSpec((1, LANES), lambda k: (0, k))],
        out_specs=[pl.BlockSpec((n, LANES), lambda k: (0, k)), st_spec],
        out_shape=[jax.ShapeDtypeStruct((n, D_INNER), F32),
                   jax.ShapeDtypeStruct(state.shape, F32)],
        compiler_params=_params(1, 48),
        name="ssd_step",
    )(decay[:, :SSM_HEADS].reshape(-1), state, xbc_act, xbc_act, xbc_act, dt, d_skip_cols)


def _gate_norm_kernel(y_ref, z_ref, w_ref, o_ref):
    gw = D_INNER // SSM_GROUPS
    for g in range(SSM_GROUPS):
        sl = slice(g * gw, (g + 1) * gw)
        v = y_ref[:, sl] * _silu(z_ref[:, sl])
        o_ref[:, sl] = _rms(v, w_ref[:, sl]).astype(BF16)


def _gate_norm(y, proj, w):
    t = y.shape[0]
    tm = min(t, 512)
    return pl.pallas_call(
        _gate_norm_kernel,
        grid=(t // tm,),
        in_specs=[pl.BlockSpec((tm, D_INNER), lambda i: (i, 0)),
                  pl.BlockSpec((tm, D_INNER), lambda i: (i, P_Z // D_INNER)),
                  pl.BlockSpec((1, D_INNER), lambda i: (0, 0))],
        out_specs=pl.BlockSpec((tm, D_INNER), lambda i: (i, 0)),
        out_shape=jax.ShapeDtypeStruct((t, D_INNER), BF16),
        compiler_params=_params(1, 48),
        name="gate_norm",
    )(y, proj, w)


def _merge_kernel(oa_ref, os_ref, wa_ref, ws_ref, ga_ref, gs_ref, o_ref):
    a = _dot(oa_ref[...], wa_ref[...])
    s = _dot(os_ref[...], ws_ref[...])
    o_ref[...] = (jax.nn.sigmoid(ga_ref[...]) * a + jax.nn.sigmoid(gs_ref[...]) * s).astype(BF16)


def _merge(o_attn, o_ssm, w_attn, w_ssm, proj):
    t = o_attn.shape[0]
    tm, tn = min(t, 1024), 512
    return pl.pallas_call(
        _merge_kernel,
        grid=(t // tm, D_MODEL // tn),
        in_specs=[pl.BlockSpec((tm, HEADS * V_DIM), lambda i, j: (i, 0)),
                  pl.BlockSpec((tm, D_INNER), lambda i, j: (i, 0)),
                  pl.BlockSpec((HEADS * V_DIM, tn), lambda i, j: (0, j)),
                  pl.BlockSpec((D_INNER, tn), lambda i, j: (0, j)),
                  pl.BlockSpec((tm, tn), lambda i, j: (i, P_GA // tn + j)),
                  pl.BlockSpec((tm, tn), lambda i, j: (i, P_GS // tn + j))],
        out_specs=pl.BlockSpec((tm, tn), lambda i, j: (i, j)),
        out_shape=jax.ShapeDtypeStruct((t, D_MODEL), BF16),
        compiler_params=_params(2, 48),
        name="merge",
    )(o_attn, o_ssm, w_attn, w_ssm, proj, proj)


def _out_proj_kernel(m_ref, w_ref, h_ref, npost_ref, npre_ref, h1_ref, hn_ref):
    h1 = h_ref[...] + _rms(_dot(m_ref[...], w_ref[...]), npost_ref[...])
    h1_ref[...] = h1
    hn_ref[...] = _rms(h1, npre_ref[...]).astype(BF16)


def _out_proj(merged, w_out, h, n_post, n_pre):
    t = h.shape[0]
    tm = min(t, 512)
    row = lambda i: (i, 0)
    fixed = lambda i: (0, 0)
    return pl.pallas_call(
        _out_proj_kernel,
        grid=(t // tm,),
        in_specs=[pl.BlockSpec((tm, D_MODEL), row), pl.BlockSpec((D_MODEL, D_MODEL), fixed),
                  pl.BlockSpec((tm, D_MODEL), row), pl.BlockSpec((1, D_MODEL), fixed),
                  pl.BlockSpec((1, D_MODEL), fixed)],
        out_specs=[pl.BlockSpec((tm, D_MODEL), row), pl.BlockSpec((tm, D_MODEL), row)],
        out_shape=[jax.ShapeDtypeStruct((t, D_MODEL), F32), jax.ShapeDtypeStruct((t, D_MODEL), BF16)],
        compiler_params=_params(1, 56),
        name="out_proj",
    )(merged, w_out, h, n_post, n_pre)


FFN_HALO = BF16_SUBLANES
FFN_ROW_CHUNKS = 2


def _ffn_kernel(x_ref, halo_ref, wg_ref, wv_ref, cwg_ref, cwv_ref, cbg_ref, cbv_ref, wd_ref, h_ref, nw_ref,
                o_ref, tg_ref, tv_ref, xe_s, u_s, acc_s):
    i, j = pl.program_id(0), pl.program_id(1)
    tm, tf = x_ref.shape[0], wg_ref.shape[1]

    @pl.when(j == 0)
    def _():
        halo = halo_ref[...]
        xe_s[0:FFN_HALO, :] = jnp.where(i == 0, jnp.zeros_like(halo), halo)
        xe_s[FFN_HALO:, :] = x_ref[...]
        acc_s[...] = jnp.zeros(acc_s.shape, F32)

    base = FFN_HALO - (FFN_CONV - 1)
    rc = tm // FFN_ROW_CHUNKS
    for c in range(FFN_ROW_CHUNKS):
        r0 = 0 if c == 0 else FFN_HALO + c * rc
        r1 = FFN_HALO + (c + 1) * rc
        xe = xe_s[r0:r1, :]
        u_s[r0:r1, 0:tf] = _dot(xe, wg_ref[...])
        u_s[r0:r1, tf:] = _dot(xe, wv_ref[...])
        t0 = base + c * rc

        def conv(lo, w_ref, b_ref):
            w = w_ref[...]
            y = u_s[t0:t0 + rc, lo:lo + tf] * w[0:1]
            for k in range(1, FFN_CONV):
                y = y + u_s[t0 + k:t0 + k + rc, lo:lo + tf] * w[k:k + 1]
            return y + b_ref[...]

        f = (_gelu_tanh(conv(0, cwg_ref, cbg_ref)) * conv(tf, cwv_ref, cbv_ref)).astype(BF16)
        acc_s[c * rc:(c + 1) * rc, :] += _dot(f, wd_ref[...])

    tail = FFN_HALO + tm - SUBLANES
    tg_ref[...] = u_s[tail:tail + SUBLANES, 0:tf]
    tv_ref[...] = u_s[tail:tail + SUBLANES, tf:]

    @pl.when(j == pl.num_programs(1) - 1)
    def _():
        o_ref[...] = h_ref[...] + _rms(acc_s[...], nw_ref[...])


def _ffn(hn, h, w_up, conv_w, conv_b, w_down, nw):
    t = h.shape[0]
    tm, tf = min(t, 512), 1024
    once = pl.Buffered(1)
    nj = D_FF // tf
    gate = lambda i, j: (0, j)
    value = lambda i, j: (0, nj + j)
    row = lambda i, j: (i, 0)
    return pl.pallas_call(
        _ffn_kernel,
        grid=(t // tm, nj),
        in_specs=[pl.BlockSpec((tm, D_MODEL), row),
                  pl.BlockSpec((FFN_HALO, D_MODEL), lambda i, j: (jnp.maximum(i * (tm // FFN_HALO) - 1, 0), 0)),
                  pl.BlockSpec((D_MODEL, tf), gate), pl.BlockSpec((D_MODEL, tf), value),
                  pl.BlockSpec((FFN_CONV, tf), gate), pl.BlockSpec((FFN_CONV, tf), value),
                  pl.BlockSpec((1, tf), gate), pl.BlockSpec((1, tf), value),
                  pl.BlockSpec((tf, D_MODEL), lambda i, j: (j, 0)),
                  pl.BlockSpec((tm, D_MODEL), row, pipeline_mode=once),
                  pl.BlockSpec((1, D_MODEL), lambda i, j: (0, 0))],
        out_specs=[pl.BlockSpec((tm, D_MODEL), row, pipeline_mode=once),
                   pl.BlockSpec((SUBLANES, tf), lambda i, j: (i, j)),
                   pl.BlockSpec((SUBLANES, tf), lambda i, j: (i, j))],
        out_shape=[jax.ShapeDtypeStruct((t, D_MODEL), F32),
                   jax.ShapeDtypeStruct((t // tm * SUBLANES, D_FF), F32),
                   jax.ShapeDtypeStruct((t // tm * SUBLANES, D_FF), F32)],
        scratch_shapes=[pltpu.VMEM((tm + FFN_HALO, D_MODEL), BF16),
                        pltpu.VMEM((tm + FFN_HALO, 2 * tf), F32),
                        pltpu.VMEM((tm, D_MODEL), F32)],
        compiler_params=_params(2, 58),
        name="ffn",
    )(hn, hn, w_up, w_up, conv_w, conv_w, conv_b, conv_b, w_down, h, nw)


def _ffn_step_kernel(x_ref, wg_ref, wv_ref, sg_ref, sv_ref, cwg_ref, cwv_ref, cbg_ref, cbv_ref, wd_ref, h_ref,
                     nw_ref, o_ref, ug_ref, uv_ref, acc_s):
    j = pl.program_id(0)
    x = x_ref[...]

    def conv(u, s_ref, w_ref, b_ref):
        w = w_ref[...]
        y = s_ref[0] * w[0:1]
        for k in range(1, FFN_CONV - 1):
            y = y + s_ref[k] * w[k:k + 1]
        return y + u * w[FFN_CONV - 1:FFN_CONV] + b_ref[...]

    ug = _dot(x, wg_ref[...])
    uv = _dot(x, wv_ref[...])
    ug_ref[...] = ug
    uv_ref[...] = uv
    f = (_gelu_tanh(conv(ug, sg_ref, cwg_ref, cbg_ref)) * conv(uv, sv_ref, cwv_ref, cbv_ref)).astype(BF16)
    d = _dot(f, wd_ref[...])

    @pl.when(j == 0)
    def _():
        acc_s[...] = d

    @pl.when(j > 0)
    def _():
        acc_s[...] += d

    @pl.when(j == pl.num_programs(0) - 1)
    def _():
        o_ref[...] = h_ref[...] + _rms(acc_s[...], nw_ref[...])


def _ffn_step(hn, h, state_t, w_up, conv_w, conv_b, w_down, nw):
    n = h.shape[0]
    tf = 512
    nj = D_FF // tf
    gate = lambda j: (0, j)
    value = lambda j: (0, nj + j)
    fixed = lambda j: (0, 0)
    return pl.pallas_call(
        _ffn_step_kernel,
        grid=(nj,),
        in_specs=[pl.BlockSpec((n, D_MODEL), fixed),
                  pl.BlockSpec((D_MODEL, tf), gate), pl.BlockSpec((D_MODEL, tf), value),
                  pl.BlockSpec((FFN_CONV - 1, n, tf), lambda j: (0, 0, j)),
                  pl.BlockSpec((FFN_CONV - 1, n, tf), lambda j: (0, 0, nj + j)),
                  pl.BlockSpec((FFN_CONV, tf), gate), pl.BlockSpec((FFN_CONV, tf), value),
                  pl.BlockSpec((1, tf), gate), pl.BlockSpec((1, tf), value),
                  pl.BlockSpec((tf, D_MODEL), lambda j: (j, 0)),
                  pl.BlockSpec((n, D_MODEL), fixed),
                  pl.BlockSpec((1, D_MODEL), fixed)],
        out_specs=[pl.BlockSpec((n, D_MODEL), fixed), pl.BlockSpec((n, tf), gate), pl.BlockSpec((n, tf), gate)],
        out_shape=[jax.ShapeDtypeStruct((n, D_MODEL), F32),
                   jax.ShapeDtypeStruct((n, D_FF), F32), jax.ShapeDtypeStruct((n, D_FF), F32)],
        scratch_shapes=[pltpu.VMEM((n, D_MODEL), F32)],
        compiler_params=_params(1, 32),
        name="ffn_step",
    )(hn, w_up, w_up, state_t, state_t, conv_w, conv_w, conv_b, conv_b, w_down, h, nw)


def _ple_kernel(h_ref, pe_ref, wg_ref, wp_ref, nw_ref, o_ref):
    h = h_ref[...]
    gate = jax.nn.sigmoid(_dot(h.astype(BF16), wg_ref[...]))
    o_ref[...] = h + _rms(gate * _dot(pe_ref[...].astype(BF16), wp_ref[...]), nw_ref[...])


def _ple(h, pe, w_gate, w_proj, nw):
    t = h.shape[0]
    tm = min(t, 512)
    row = lambda i: (i, 0)
    fixed = lambda i: (0, 0)
    return pl.pallas_call(
        _ple_kernel,
        grid=(t // tm,),
        in_specs=[pl.BlockSpec((tm, D_MODEL), row), pl.BlockSpec((tm, PLE_DIM), row),
                  pl.BlockSpec((D_MODEL, D_MODEL), fixed), pl.BlockSpec((PLE_DIM, D_MODEL), fixed),
                  pl.BlockSpec((1, D_MODEL), fixed)],
        out_specs=pl.BlockSpec((tm, D_MODEL), row),
        out_shape=jax.ShapeDtypeStruct((t, D_MODEL), F32),
        compiler_params=_params(1, 56),
        name="ple",
    )(h, pe, w_gate, w_proj, nw)


def _rot_half_cols(w):
    half = QK_ROPE // 2
    return jnp.concatenate([-w[..., half:], w[..., :half]], axis=-1)


def _prepare_weights(w):
    w_in_t = w['w_in'].T
    splits = np.cumsum([0, Q_LORA, KV_LORA, QK_ROPE, D_INNER, CONV_DIM, SSM_HEADS, D_MODEL, D_MODEL])
    q_c, kv_c, k_r, z, xbc, dt, g_a, g_s = [w_in_t[a:b] for a, b in zip(splits[:-1], splits[1:])]
    pad = jnp.zeros((SMALL_W - 2 * QK_ROPE - SSM_HEADS, D_MODEL), F32)
    k_r_rot = _rot_half_cols(k_r.T).T
    packed_t = jnp.concatenate([z, g_a, g_s, xbc, kv_c, k_r, k_r_rot, dt, pad, q_c], axis=0)
    wq = w['w_q_up'].reshape(Q_LORA, HEADS, QK_NOPE + QK_ROPE)
    wq_rope = wq[:, :, QK_NOPE:]
    wkv = w['w_kv_up'].reshape(KV_LORA, HEADS, QK_NOPE + V_DIM)
    lane_pad = LANES - SSM_HEADS
    return dict(
        w_in=packed_t.astype(BF16),
        wq_nope=wq[:, :, :QK_NOPE].reshape(Q_LORA, HEADS * QK_NOPE).astype(BF16),
        wq_rope=wq_rope.reshape(Q_LORA, HEADS * QK_ROPE).astype(BF16),
        wq_rope_rot=_rot_half_cols(wq_rope).reshape(Q_LORA, HEADS * QK_ROPE).astype(BF16),
        w_nope=wkv[:, :, :QK_NOPE].transpose(1, 2, 0).astype(BF16),
        w_v=wkv[:, :, QK_NOPE:].transpose(1, 0, 2).astype(BF16),
        w_branch_attn=w['w_branch_attn'].astype(BF16),
        w_branch_ssm=w['w_branch_ssm'].astype(BF16),
        w_out=w['w_out'].astype(BF16),
        w_ffn_up=w['w_ffn_up'].astype(BF16),
        w_ffn_down=w['w_ffn_down'].astype(BF16),
        w_ple_gate=w['w_ple_gate'].astype(BF16),
        w_ple_proj=w['w_ple_proj'].astype(BF16),
        dt_bias=jnp.pad(w['dt_bias'], (0, lane_pad))[None, :],
        a_neg=jnp.pad(-jnp.exp(w['a_log']), (0, lane_pad))[None, :],
        d_skip=jnp.repeat(w['d_skip'], SSM_P)[None, :],
    )


def _rope_tables(pos):
    half = QK_ROPE // 2
    inv = ROPE_THETA ** (-jnp.arange(half, dtype=F32) / half)
    ang = pos[:, None] * inv[None, :]
    cos, sin = jnp.cos(ang), jnp.sin(ang)
    cos64, sin64 = jnp.concatenate([cos, cos], axis=1), jnp.concatenate([sin, sin], axis=1)
    return cos64, sin64, jnp.concatenate([cos64, cos64], axis=1), jnp.concatenate([sin64, sin64], axis=1)


def _vec(v):
    return v[None, :]


def _mixer_front(h, pos, w, pw):
    proj = _in_proj(h, _vec(w['norm_pre_mix']), pw['w_in'])
    cos64, sin64, cos128, sin128 = _rope_tables(pos)
    c, c_bf, kr, kr_bf = _kv_prep(proj, cos64, sin64, _vec(w['norm_kv']))
    q_nope, q_rope = _q_prep(proj, _vec(w['norm_q']), pw['wq_nope'], pw['wq_rope'], pw['wq_rope_rot'],
                             cos128, sin128)
    return proj, c, c_bf, kr, kr_bf, q_nope, q_rope


def _mixer_back(h, proj, o_attn, o_ssm, w, pw):
    merged = _merge(o_attn, o_ssm, pw['w_branch_attn'], pw['w_branch_ssm'], proj)
    return _out_proj(merged, pw['w_out'], h, _vec(w['norm_post_mix']), _vec(w['norm_pre_ffn']))


def _prompt_layer(h, pe, w, pw):
    t = h.shape[0]
    pos = jnp.arange(t, dtype=F32)
    proj, c, c_bf, kr, kr_bf, q_nope, q_rope = _mixer_front(h, pos, w, pw)
    o_attn = _prompt_attn(q_nope, q_rope, pw['w_nope'], pw['w_v'], c_bf, kr_bf)
    xbc_act = _ssm_conv(proj, w['conv_ssm_w'], _vec(w['conv_ssm_b']))
    o_ssm, st = _ssd_chunked(xbc_act, proj, pw['dt_bias'], pw['a_neg'], pw['d_skip'], _vec(w['norm_ssm']))
    h1, hn1 = _mixer_back(h, proj, o_attn, o_ssm, w, pw)
    h2, tail_g, tail_v = _ffn(hn1, h1, pw['w_ffn_up'], w['conv_ffn_w'], _vec(w['conv_ffn_b']),
                              pw['w_ffn_down'], _vec(w['norm_post_ffn']))
    h3 = _ple(h2, pe, pw['w_ple_gate'], pw['w_ple_proj'], _vec(w['norm_ple']))
    ssm_conv_new = proj[t - (SSM_CONV - 1):, P_XBC:P_XBC + CONV_DIM]
    last = tail_g.shape[0] - (FFN_CONV - 1)
    ffn_conv_new = jnp.concatenate([tail_g[last:], tail_v[last:]], axis=1)
    return h3, (c, kr, st.reshape(SSM_HEADS, SSM_P, SSM_N), ssm_conv_new, ffn_conv_new)


def _sample_layer(h, pe, past, w, pw, cache_c, cache_kr, layer, page_table, ssm_state, ssm_conv_state,
                  ffn_conv_state):
    n = h.shape[0]
    pos = jnp.full((n,), past, F32)
    proj, c, c_bf, kr, kr_bf, q_nope, q_rope = _mixer_front(h, pos, w, pw)
    q_lat = _q_latent_sample(q_nope, pw['w_nope']).transpose(1, 0, 2)
    o_lat = _decode_attn(page_table, q_lat, q_rope.reshape(n, HEADS, QK_ROPE), c_bf[:, None, :],
                         kr_bf[:, None, :], cache_c, jnp.swapaxes(cache_kr, 2, 3), layer)
    o_attn = _o_attn_sample(o_lat.transpose(1, 0, 2), pw['w_v'])
    xbc_act = _ssm_conv_step(ssm_conv_state.transpose(1, 0, 2), proj, w['conv_ssm_w'], _vec(w['conv_ssm_b']))
    dt, decay = _dt_prep(proj, pw['dt_bias'], pw['a_neg'])
    y_ssm, st = _ssd_step(ssm_state, xbc_act, dt, decay, pw['d_skip'])
    o_ssm = _gate_norm(y_ssm, proj, _vec(w['norm_ssm']))
    h1, hn1 = _mixer_back(h, proj, o_attn, o_ssm, w, pw)
    h2, u_g, u_v = _ffn_step(hn1, h1, ffn_conv_state.transpose(1, 0, 2), pw['w_ffn_up'], w['conv_ffn_w'],
                             _vec(w['conv_ffn_b']), pw['w_ffn_down'], _vec(w['norm_post_ffn']))
    h3 = _ple(h2, pe, pw['w_ple_gate'], pw['w_ple_proj'], _vec(w['norm_ple']))
    xbc_raw = proj[:, P_XBC:P_XBC + CONV_DIM]
    ssm_conv_new = jnp.concatenate([ssm_conv_state[:, 1:], xbc_raw[:, None, :]], axis=1)
    u = jnp.concatenate([u_g, u_v], axis=1)
    ffn_conv_new = jnp.concatenate([ffn_conv_state[:, 1:], u[:, None, :]], axis=1)
    return h3, (c[:, None, :], kr[:, None, :], st, ssm_conv_new, ffn_conv_new)


def kernel(x_prompt, x_sample, cache_kv_latent, cache_k_rope, state_ssm, state_ssm_conv, state_ffn_conv, page_table, p_prompt, p_sample, norm_pre_mix, w_in, norm_q, w_q_up, norm_kv, w_kv_up, w_branch_attn, conv_ssm_w, conv_ssm_b, dt_bias, a_log, d_skip, norm_ssm, w_branch_ssm, w_out, norm_post_mix, norm_pre_ffn, w_ffn_up, conv_ffn_w, conv_ffn_b, w_ffn_down, norm_post_ffn, w_ple_gate, w_ple_proj, norm_ple):
    assert x_prompt.shape[0] == 1, "one prompt sequence"
    assert x_sample.shape[1] == 1, "one new token per sample sequence"
    depth = w_in.shape[0]
    past = page_table.shape[1] * PAGE
    weights = dict(norm_pre_mix=norm_pre_mix, w_in=w_in, norm_q=norm_q, w_q_up=w_q_up, norm_kv=norm_kv,
                   w_kv_up=w_kv_up, w_branch_attn=w_branch_attn, conv_ssm_w=conv_ssm_w, conv_ssm_b=conv_ssm_b,
                   dt_bias=dt_bias, a_log=a_log, d_skip=d_skip, norm_ssm=norm_ssm, w_branch_ssm=w_branch_ssm,
                   w_out=w_out, norm_post_mix=norm_post_mix, norm_pre_ffn=norm_pre_ffn, w_ffn_up=w_ffn_up,
                   conv_ffn_w=conv_ffn_w, conv_ffn_b=conv_ffn_b, w_ffn_down=w_ffn_down,
                   norm_post_ffn=norm_post_ffn, w_ple_gate=w_ple_gate, w_ple_proj=w_ple_proj, norm_ple=norm_ple)
    hp, hs = x_prompt[0], x_sample[:, 0]
    outs_p, outs_s = [], []
    for d in range(depth):
        w = {k: v[d] for k, v in weights.items()}
        pw = _prepare_weights(w)
        hp, st_p = _prompt_layer(hp, p_prompt[d, 0], w, pw)
        hs, st_s = _sample_layer(hs, p_sample[d, :, 0], past, w, pw, cache_kv_latent, cache_k_rope, d,
                                 page_table, state_ssm[d], state_ssm_conv[d], state_ffn_conv[d])
        outs_p.append(st_p)
        outs_s.append(st_s)
    stack_p = [jnp.stack([o[k] for o in outs_p])[:, None] for k in range(5)]
    stack_s = [jnp.stack([o[k] for o in outs_s]) for k in range(5)]
    return (hp[None], hs[:, None], *stack_p, *stack_s)
```

```python
import functools

import jax
import jax.numpy as jnp
import numpy as np
from jax import lax
from jax.experimental import pallas as pl
from jax.experimental.pallas import tpu as pltpu

F32, BF16 = jnp.float32, jnp.bfloat16
HIGHEST = lax.Precision.HIGHEST

LANES = 128
SUBLANES = 8
BF16_SUBLANES = 16
V7X_VMEM_BYTES = 64 * 1024 * 1024

D_MODEL = 2048
PAGE = 128
HEADS = 16
QK_NOPE = 128
QK_ROPE = 64
V_DIM = 128
Q_LORA = 768
KV_LORA = 512
ROPE_THETA = 10000.0
ATTN_SCALE = (QK_NOPE + QK_ROPE) ** -0.5
SOFTMAX_LOG2_SCALE = ATTN_SCALE * float(np.log2(np.e))
SSM_HEADS = 32
SSM_P = 64
D_INNER = SSM_HEADS * SSM_P
SSM_GROUPS = 4
SSM_N = 128
SSM_CONV = 4
GN = SSM_GROUPS * SSM_N
CONV_DIM = D_INNER + 2 * GN
CHUNK = 128
PAIRS = SSM_HEADS // 2
PAIRS_PER_GROUP = PAIRS // SSM_GROUPS
D_FF = 4 * D_MODEL
FFN_CONV = 3
PLE_DIM = 256
EPS = 1e-6

P_Z = 0
P_GA = P_Z + D_INNER
P_GS = P_GA + D_MODEL
P_XBC = P_GS + D_MODEL
P_KV = P_XBC + CONV_DIM
P_SMALL = P_KV + KV_LORA
P_Q = P_SMALL + 2 * LANES
P_COLS = P_Q + Q_LORA
SMALL_W = 2 * LANES


def _params(n_axes, vmem_mib):
    return pltpu.CompilerParams(dimension_semantics=("arbitrary",) * n_axes,
                                vmem_limit_bytes=vmem_mib * 1024 * 1024)


def _rms(x, w):
    return x * lax.rsqrt(jnp.mean(x * x, axis=-1, keepdims=True) + EPS) * w


def _dot(a, b):
    return jnp.dot(a, b, preferred_element_type=F32)


def _dot_nt(a, b):
    return lax.dot_general(a, b, (((1,), (1,)), ((), ())), preferred_element_type=F32)


def _dot_exact(a, b):
    return jnp.dot(a, b, precision=HIGHEST, preferred_element_type=F32)


def _rep(x, n):
    return x if n == 1 else jnp.concatenate([x] * n, axis=1)


def _silu(x):
    return x * jax.nn.sigmoid(x)


def _softplus(x):
    return jnp.maximum(x, 0.0) + jnp.log1p(jnp.exp(-jnp.abs(x)))


def _gelu_tanh(x):
    return x * (0.5 * (1.0 + jnp.tanh(np.sqrt(2.0 / np.pi).astype(np.float32) * (x + 0.044715 * (x * x * x)))))


def _in_proj_kernel(x_ref, nw_ref, w_ref, o_ref, xn_ref):
    @pl.when(pl.program_id(1) == 0)
    def _():
        xn_ref[...] = _rms(x_ref[...], nw_ref[...]).astype(BF16)

    o_ref[...] = _dot_nt(xn_ref[...], w_ref[...])


def _in_proj(x, nw, w_packed_t):
    t = x.shape[0]
    tm, tn = min(t, 1024), 1536
    return pl.pallas_call(
        _in_proj_kernel,
        grid=(t // tm, P_COLS // tn),
        in_specs=[pl.BlockSpec((tm, D_MODEL), lambda i, j: (i, 0)),
                  pl.BlockSpec((1, D_MODEL), lambda i, j: (0, 0)),
                  pl.BlockSpec((tn, D_MODEL), lambda i, j: (j, 0))],
        out_specs=pl.BlockSpec((tm, tn), lambda i, j: (i, j)),
        out_shape=jax.ShapeDtypeStruct((t, P_COLS), F32),
        scratch_shapes=[pltpu.VMEM((tm, D_MODEL), BF16)],
        compiler_params=_params(2, 56),
        name="in_proj",
    )(x, nw, w_packed_t)


def _kv_prep_kernel(kv_ref, sm_ref, cos_ref, sin_ref, nw_ref, c_ref, cbf_ref, kr_ref, krbf_ref):
    c = _rms(kv_ref[...], nw_ref[...])
    c_ref[...] = c
    cbf_ref[...] = c.astype(BF16)
    sm = sm_ref[...]
    kr = sm[:, 0:QK_ROPE] * cos_ref[...] + sm[:, QK_ROPE:2 * QK_ROPE] * sin_ref[...]
    kr_ref[...] = kr
    krbf_ref[...] = kr.astype(BF16)


def _kv_prep(proj, cos64, sin64, nw):
    t = proj.shape[0]
    tm = min(t, 512)
    return pl.pallas_call(
        _kv_prep_kernel,
        grid=(t // tm,),
        in_specs=[pl.BlockSpec((tm, KV_LORA), lambda i: (i, P_KV // KV_LORA)),
                  pl.BlockSpec((tm, SMALL_W), lambda i: (i, P_SMALL // SMALL_W)),
                  pl.BlockSpec((tm, QK_ROPE), lambda i: (i, 0)),
                  pl.BlockSpec((tm, QK_ROPE), lambda i: (i, 0)),
                  pl.BlockSpec((1, KV_LORA), lambda i: (0, 0))],
        out_specs=[pl.BlockSpec((tm, KV_LORA), lambda i: (i, 0)),
                   pl.BlockSpec((tm, KV_LORA), lambda i: (i, 0)),
                   pl.BlockSpec((tm, QK_ROPE), lambda i: (i, 0)),
                   pl.BlockSpec((tm, QK_ROPE), lambda i: (i, 0))],
        out_shape=[jax.ShapeDtypeStruct((t, KV_LORA), F32), jax.ShapeDtypeStruct((t, KV_LORA), BF16),
                   jax.ShapeDtypeStruct((t, QK_ROPE), F32), jax.ShapeDtypeStruct((t, QK_ROPE), BF16)],
        compiler_params=_params(1, 32),
        name="kv_prep",
    )(proj, proj, cos64, sin64, nw)


def _q_prep_kernel(q_ref, nw_ref, wn_ref, wr_ref, wrr_ref, cos_ref, sin_ref, qn_ref, qr_ref):
    qn = _rms(q_ref[...], nw_ref[...]).astype(BF16)
    qn_ref[...] = _dot(qn, wn_ref[...]).astype(BF16)
    r = _dot(qn, wr_ref[...])
    rr = _dot(qn, wrr_ref[...])
    cos, sin = cos_ref[...], sin_ref[...]
    for k in range(HEADS * QK_ROPE // LANES):
        sl = slice(k * LANES, (k + 1) * LANES)
        qr_ref[:, sl] = (r[:, sl] * cos + rr[:, sl] * sin).astype(BF16)


def _q_prep(proj, nw, wq_nope, wq_rope, wq_rope_rot, cos128, sin128):
    t = proj.shape[0]
    tm = min(t, 512)
    n_nope, n_rope = HEADS * QK_NOPE, HEADS * QK_ROPE
    return pl.pallas_call(
        _q_prep_kernel,
        grid=(t // tm,),
        in_specs=[pl.BlockSpec((tm, Q_LORA), lambda i: (i, P_Q // Q_LORA)),
                  pl.BlockSpec((1, Q_LORA), lambda i: (0, 0)),
                  pl.BlockSpec((Q_LORA, n_nope), lambda i: (0, 0)),
                  pl.BlockSpec((Q_LORA, n_rope), lambda i: (0, 0)),
                  pl.BlockSpec((Q_LORA, n_rope), lambda i: (0, 0)),
                  pl.BlockSpec((tm, LANES), lambda i: (i, 0)),
                  pl.BlockSpec((tm, LANES), lambda i: (i, 0))],
        out_specs=[pl.BlockSpec((tm, n_nope), lambda i: (i, 0)),
                   pl.BlockSpec((tm, n_rope), lambda i: (i, 0))],
        out_shape=[jax.ShapeDtypeStruct((t, n_nope), BF16), jax.ShapeDtypeStruct((t, n_rope), BF16)],
        compiler_params=_params(1, 48),
        name="q_prep",
    )(proj, nw, wq_nope, wq_rope, wq_rope_rot, cos128, sin128)


def _prompt_attn_kernel(qi_ref, kj_ref, qn_ref, qr_ref, wn_ref, wv_ref, c_ref, kr_ref, o_ref,
                        qlat_s, qrope_s, m_s, l_s, acc_s, *, tq):
    step = pl.program_id(0)
    i = qi_ref[step]
    j = kj_ref[step]
    j_last = i // 4
    r = i % 4

    @pl.when(j == 0)
    def _init():
        for h in range(HEADS):
            rs = slice(h * tq, (h + 1) * tq)
            qlat_s[rs, :] = _dot(qn_ref[:, h * QK_NOPE:(h + 1) * QK_NOPE], wn_ref[h]).astype(BF16)
            qrope_s[rs, :] = qr_ref[:, h * QK_ROPE:(h + 1) * QK_ROPE]
        m_s[...] = jnp.full(m_s.shape, -jnp.inf, F32)
        l_s[...] = jnp.zeros(l_s.shape, F32)
        acc_s[...] = jnp.zeros(acc_s.shape, F32)

    def scores(k0, nk):
        return (_dot_nt(qlat_s[...], c_ref[pl.ds(k0, nk), :])
                + _dot_nt(qrope_s[...], kr_ref[pl.ds(k0, nk), :])) * SOFTMAX_LOG2_SCALE

    def update(s, k0, nk, limit=None):
        if limit is not None:
            visible = (lax.broadcasted_iota(jnp.int32, (1, tq, nk), 2)
                       <= lax.broadcasted_iota(jnp.int32, (1, tq, nk), 1) + limit)
            s = jnp.where(visible, s.reshape(HEADS, tq, nk), -jnp.inf).reshape(HEADS * tq, nk)
        m_prev = m_s[...]
        m_new = jnp.maximum(m_prev, jnp.max(s, axis=-1, keepdims=True))
        alpha = jnp.exp2(m_prev - m_new)
        p = jnp.exp2(s - _rep(m_new, nk // LANES))
        l_s[...] = alpha * l_s[...] + jnp.sum(p, axis=-1, keepdims=True)
        acc_s[...] = acc_s[...] * _rep(alpha, KV_LORA // LANES) + _dot(p.astype(BF16), c_ref[pl.ds(k0, nk), :])
        m_s[...] = m_new

    @pl.when(j < j_last)
    def _below_diagonal():
        s_a = scores(0, 2 * tq)
        s_b = scores(2 * tq, 2 * tq)
        update(s_a, 0, 2 * tq)
        update(s_b, 2 * tq, 2 * tq)

    @pl.when(j == j_last)
    def _diagonal_block():
        def quarter(q, carry):
            k0 = pl.multiple_of(q * tq, tq)
            update(scores(k0, tq), k0, tq, limit=(r - q) * tq)
            return carry

        lax.fori_loop(0, r + 1, quarter, 0)

    @pl.when(j == j_last)
    def _finish():
        for h in range(HEADS):
            rs = slice(h * tq, (h + 1) * tq)
            o_lat = (acc_s[rs, :] / _rep(l_s[rs, :], KV_LORA // LANES)).astype(BF16)
            o_ref[:, h * V_DIM:(h + 1) * V_DIM] = _dot(o_lat, wv_ref[h]).astype(BF16)


def _prompt_attn(q_nope, q_rope, w_nope, w_v, c_bf, kr_bf):
    t = q_nope.shape[0]
    tq = 256
    tk = 4 * tq
    assert t % tk == 0, "prompt length must be a multiple of the key block"
    pairs = [(i, j) for i in range(t // tq) for j in range(i // 4 + 1)]
    qi = jnp.asarray([p[0] for p in pairs], jnp.int32)
    kj = jnp.asarray([p[1] for p in pairs], jnp.int32)
    rows = HEADS * tq
    grid_spec = pltpu.PrefetchScalarGridSpec(
        num_scalar_prefetch=2,
        grid=(len(pairs),),
        in_specs=[pl.BlockSpec((tq, HEADS * QK_NOPE), lambda s, qi, kj: (qi[s], 0)),
                  pl.BlockSpec((tq, HEADS * QK_ROPE), lambda s, qi, kj: (qi[s], 0)),
                  pl.BlockSpec((HEADS, QK_NOPE, KV_LORA), lambda s, qi, kj: (0, 0, 0)),
                  pl.BlockSpec((HEADS, KV_LORA, V_DIM), lambda s, qi, kj: (0, 0, 0)),
                  pl.BlockSpec((tk, KV_LORA), lambda s, qi, kj: (kj[s], 0)),
                  pl.BlockSpec((tk, QK_ROPE), lambda s, qi, kj: (kj[s], 0))],
        out_specs=pl.BlockSpec((tq, HEADS * V_DIM), lambda s, qi, kj: (qi[s], 0)),
        scratch_shapes=[pltpu.VMEM((rows, KV_LORA), BF16), pltpu.VMEM((rows, QK_ROPE), BF16),
                        pltpu.VMEM((rows, LANES), F32), pltpu.VMEM((rows, LANES), F32),
                        pltpu.VMEM((rows, KV_LORA), F32)],
    )
    return pl.pallas_call(
        functools.partial(_prompt_attn_kernel, tq=tq),
        grid_spec=grid_spec,
        out_shape=jax.ShapeDtypeStruct((t, HEADS * V_DIM), BF16),
        compiler_params=_params(1, 56),
        name="prompt_attn",
    )(qi, kj, q_nope, q_rope, w_nope, w_v, c_bf, kr_bf)


def _head_proj_in_kernel(x_ref, w_ref, o_ref):
    o_ref[0] = _dot(x_ref[...], w_ref[0]).astype(BF16)


def _q_latent_sample(q_nope, w_nope):
    n = q_nope.shape[0]
    return pl.pallas_call(
        _head_proj_in_kernel,
        grid=(HEADS,),
        in_specs=[pl.BlockSpec((n, QK_NOPE), lambda h: (0, h)),
                  pl.BlockSpec((1, QK_NOPE, KV_LORA), lambda h: (h, 0, 0))],
        out_specs=pl.BlockSpec((1, n, KV_LORA), lambda h: (h, 0, 0)),
        out_shape=jax.ShapeDtypeStruct((HEADS, n, KV_LORA), BF16),
        compiler_params=_params(1, 32),
        name="q_latent_sample",
    )(q_nope, w_nope)


def _head_proj_out_kernel(x_ref, w_ref, o_ref):
    o_ref[...] = _dot(x_ref[0], w_ref[0]).astype(BF16)


def _o_attn_sample(o_lat, w_v):
    n = o_lat.shape[1]
    return pl.pallas_call(
        _head_proj_out_kernel,
        grid=(HEADS,),
        in_specs=[pl.BlockSpec((1, n, KV_LORA), lambda h: (h, 0, 0)),
                  pl.BlockSpec((1, KV_LORA, V_DIM), lambda h: (h, 0, 0))],
        out_specs=pl.BlockSpec((n, V_DIM), lambda h: (0, h)),
        out_shape=jax.ShapeDtypeStruct((n, HEADS * V_DIM), BF16),
        compiler_params=_params(1, 32),
        name="o_attn_sample",
    )(o_lat, w_v)


DECODE_SLOTS = 4


def _decode_attn_kernel(pt_ref, qlat_ref, qr_ref, cnew_ref, krnew_ref, cache_c, cache_krt, o_ref,
                        cbuf, krbuf, sems, *, layer, n_seq, n_pages, group):
    n = pl.program_id(0)
    n_chunks = n_pages // group
    keys = group * PAGE
    ahead = DECODE_SLOTS - 1

    def page_copies(seq, chunk, g):
        slot = chunk % DECODE_SLOTS
        page = pt_ref[seq * n_pages + chunk * group + g]
        return (pltpu.make_async_copy(cache_c.at[layer, page], cbuf.at[slot, g], sems.at[0, slot]),
                pltpu.make_async_copy(cache_krt.at[layer, page], krbuf.at[slot, :, pl.ds(g * PAGE, PAGE)],
                                      sems.at[1, slot]))

    def start(seq, chunk):
        for g in range(group):
            for cp in page_copies(seq, chunk, g):
                cp.start()

    def wait(seq, chunk):
        for g in range(group):
            for cp in page_copies(seq, chunk, g):
                cp.wait()

    @pl.when(n == 0)
    def _():
        for ch in range(ahead):
            start(n, ch)

    q = qlat_ref[0]
    qr = qr_ref[0]
    m = jnp.full((HEADS, 1), -jnp.inf, F32)
    l = jnp.zeros((HEADS, 1), F32)
    acc = jnp.zeros((HEADS, KV_LORA), F32)
    for ch in range(n_chunks):
        nxt = ch + ahead
        if nxt < n_chunks:
            start(n, nxt)
        else:
            @pl.when(n + 1 < n_seq)
            def _():
                start(n + 1, nxt - n_chunks)
        wait(n, ch)
        slot = ch % DECODE_SLOTS
        cb = cbuf[slot].reshape(keys, KV_LORA).astype(BF16)
        krt = krbuf[slot].astype(BF16)
        s = (_dot_nt(q, cb) + _dot(qr, krt)) * SOFTMAX_LOG2_SCALE
        m_new = jnp.maximum(m, jnp.max(s, axis=-1, keepdims=True))
        alpha = jnp.exp2(m - m_new)
        p = jnp.exp2(s - m_new)
        l = alpha * l + jnp.sum(p, axis=-1, keepdims=True)
        acc = alpha * acc + _dot(p.astype(BF16), cb)
        m = m_new
    c_new = cnew_ref[0].astype(F32)
    s_new = (jnp.sum(q.astype(F32) * c_new, axis=-1, keepdims=True)
             + jnp.sum(qr.astype(F32) * krnew_ref[0].astype(F32), axis=-1, keepdims=True)) * SOFTMAX_LOG2_SCALE
    m_new = jnp.maximum(m, s_new)
    alpha = jnp.exp2(m - m_new)
    p_new = jnp.exp2(s_new - m_new)
    l = alpha * l + p_new
    acc = alpha * acc + p_new.astype(BF16).astype(F32) * c_new
    o_ref[0] = (acc / l).astype(BF16)


def _decode_attn(page_table, q_lat, q_rope, c_new, kr_new, cache_c, cache_krt, layer):
    n_seq, n_pages = page_table.shape
    group = 32
    while n_pages % (group * DECODE_SLOTS):
        group //= 2
    assert group >= 1, "pages per sequence must be a multiple of the ring depth"
    grid_spec = pltpu.PrefetchScalarGridSpec(
        num_scalar_prefetch=1,
        grid=(n_seq,),
        in_specs=[pl.BlockSpec((1, HEADS, KV_LORA), lambda n, pt: (n, 0, 0)),
                  pl.BlockSpec((1, HEADS, QK_ROPE), lambda n, pt: (n, 0, 0)),
                  pl.BlockSpec((1, 1, KV_LORA), lambda n, pt: (n, 0, 0)),
                  pl.BlockSpec((1, 1, QK_ROPE), lambda n, pt: (n, 0, 0)),
                  pl.BlockSpec(memory_space=pl.ANY),
                  pl.BlockSpec(memory_space=pl.ANY)],
        out_specs=pl.BlockSpec((1, HEADS, KV_LORA), lambda n, pt: (n, 0, 0)),
        scratch_shapes=[pltpu.VMEM((DECODE_SLOTS, group, PAGE, KV_LORA), F32),
                        pltpu.VMEM((DECODE_SLOTS, QK_ROPE, group * PAGE), F32),
                        pltpu.SemaphoreType.DMA((2, DECODE_SLOTS))],
    )
    return pl.pallas_call(
        functools.partial(_decode_attn_kernel, layer=layer, n_seq=n_seq, n_pages=n_pages, group=group),
        grid_spec=grid_spec,
        out_shape=jax.ShapeDtypeStruct((n_seq, HEADS, KV_LORA), BF16),
        compiler_params=_params(1, 56),
        name="decode_attn",
    )(page_table.reshape(-1), q_lat, q_rope, c_new, kr_new, cache_c, cache_krt)


def _ssm_conv_kernel(x_ref, halo_ref, w_ref, b_ref, o_ref, xp_s):
    tm = x_ref.shape[0]
    halo = halo_ref[...]
    xp_s[0:SUBLANES, :] = jnp.where(pl.program_id(0) == 0, jnp.zeros_like(halo), halo)
    xp_s[SUBLANES:, :] = x_ref[...]
    w = w_ref[...]
    base = SUBLANES - (SSM_CONV - 1)
    y = xp_s[base:base + tm, :] * w[0:1]
    for k in range(1, SSM_CONV):
        y = y + xp_s[base + k:base + k + tm, :] * w[k:k + 1]
    o_ref[...] = _silu(y + b_ref[...])


def _ssm_conv(proj, w, b):
    t = proj.shape[0]
    tm, tn = min(t, 512), 512
    c0 = P_XBC // tn
    return pl.pallas_call(
        _ssm_conv_kernel,
        grid=(t // tm, CONV_DIM // tn),
        in_specs=[pl.BlockSpec((tm, tn), lambda i, j: (i, c0 + j)),
                  pl.BlockSpec((SUBLANES, tn), lambda i, j: (jnp.maximum(i * (tm // SUBLANES) - 1, 0), c0 + j)),
                  pl.BlockSpec((SSM_CONV, tn), lambda i, j: (0, j)),
                  pl.BlockSpec((1, tn), lambda i, j: (0, j))],
        out_specs=pl.BlockSpec((tm, tn), lambda i, j: (i, j)),
        out_shape=jax.ShapeDtypeStruct((t, CONV_DIM), F32),
        scratch_shapes=[pltpu.VMEM((tm + SUBLANES, tn), F32)],
        compiler_params=_params(2, 32),
        name="ssm_conv",
    )(proj, proj, w, b)


def _ssm_conv_step_kernel(st_ref, x_ref, w_ref, b_ref, o_ref):
    w = w_ref[...]
    y = st_ref[0] * w[0:1]
    for k in range(1, SSM_CONV - 1):
        y = y + st_ref[k] * w[k:k + 1]
    y = y + x_ref[...] * w[SSM_CONV - 1:SSM_CONV]
    o_ref[...] = _silu(y + b_ref[...])


def _ssm_conv_step(state_t, proj, w, b):
    n = proj.shape[0]
    tn = 512
    c0 = P_XBC // tn
    return pl.pallas_call(
        _ssm_conv_step_kernel,
        grid=(CONV_DIM // tn,),
        in_specs=[pl.BlockSpec((SSM_CONV - 1, n, tn), lambda j: (0, 0, j)),
                  pl.BlockSpec((n, tn), lambda j: (0, c0 + j)),
                  pl.BlockSpec((SSM_CONV, tn), lambda j: (0, j)),
                  pl.BlockSpec((1, tn), lambda j: (0, j))],
        out_specs=pl.BlockSpec((n, tn), lambda j: (0, j)),
        out_shape=jax.ShapeDtypeStruct((n, CONV_DIM), F32),
        compiler_params=_params(1, 32),
        name="ssm_conv_step",
    )(state_t, proj, w, b)


def _ssd_chunk_kernel(xbc_ref, sm_ref, z_ref, dtb_ref, a_ref, dsk_ref, nw_ref, o_ref, hout_ref, st_s):
    c = pl.program_id(0)

    @pl.when(c == 0)
    def _():
        st_s[...] = jnp.zeros(st_s.shape, F32)

    ln = CHUNK
    dt = _softplus(sm_ref[:, LANES:2 * LANES] + dtb_ref[...])
    row = lax.broadcasted_iota(jnp.int32, (ln, ln), 0)
    col = lax.broadcasted_iota(jnp.int32, (ln, ln), 1)
    tri = row >= col
    cs = _dot_exact(tri.astype(F32), dt * a_ref[...])
    cs_t = cs.T
    dt_t = dt.T
    dtw_t = dt_t * jnp.exp(cs_t[:, ln - 1:ln] - cs_t)
    e_last = jnp.exp(cs[ln - 1:ln, :])
    lo_lane = lax.broadcasted_iota(jnp.int32, (ln, LANES), 1) < SSM_P
    lo_lane_row = lax.broadcasted_iota(jnp.int32, (1, LANES), 1) < SSM_P
    gw = D_INNER // SSM_GROUPS

    for g in range(SSM_GROUPS):
        b_f = xbc_ref[:, D_INNER + g * SSM_N:D_INNER + (g + 1) * SSM_N]
        c_f = xbc_ref[:, D_INNER + GN + g * SSM_N:D_INNER + GN + (g + 1) * SSM_N]
        cbt = _dot_nt(c_f.astype(BF16), b_f.astype(BF16))
        b_t = b_f.T
        gated = []
        for kk in range(PAIRS_PER_GROUP):
            k = g * PAIRS_PER_GROUP + kk
            sl = slice(k * LANES, (k + 1) * LANES)
            x2 = xbc_ref[:, sl]
            x2b = x2.astype(BF16)
            s_prev = st_s[k]
            rhs = jnp.concatenate([x2b, s_prev.astype(BF16)], axis=0)
            ys, sts = [], []
            for h in (2 * k, 2 * k + 1):
                cs_col = jnp.broadcast_to(cs[:, h:h + 1], (ln, ln))
                decay = jnp.exp(jnp.where(tri, cs_col - cs_t[h:h + 1, :], -jnp.inf))
                within = (cbt * decay * dt_t[h:h + 1, :]).astype(BF16)
                carried = (c_f * jnp.exp(cs_col)).astype(BF16)
                ys.append(_dot(jnp.concatenate([within, carried], axis=1), rhs))
                sts.append(_dot((b_t * dtw_t[h:h + 1, :]).astype(BF16), x2b))
            y = jnp.where(lo_lane, ys[0], ys[1]) + dsk_ref[:, sl] * x2
            keep = jnp.where(lo_lane_row, e_last[:, 2 * k:2 * k + 1], e_last[:, 2 * k + 1:2 * k + 2])
            st_s[k] = s_prev * keep + jnp.where(lo_lane, sts[0], sts[1])
            gated.append(y * _silu(z_ref[:, sl]))
        ssq = sum(jnp.sum(v * v, axis=-1, keepdims=True) for v in gated)
        scale = lax.rsqrt(ssq / gw + EPS)
        for kk, v in enumerate(gated):
            sl = slice((g * PAIRS_PER_GROUP + kk) * LANES, (g * PAIRS_PER_GROUP + kk + 1) * LANES)
            o_ref[:, sl] = (v * scale * nw_ref[:, sl]).astype(BF16)

    @pl.when(c == pl.num_programs(0) - 1)
    def _():
        for k in range(PAIRS):
            hout_ref[k] = st_s[k].T


def _ssd_chunked(xbc_act, proj, dt_bias_pad, a_pad, d_skip_cols, norm_w):
    t = xbc_act.shape[0]
    return pl.pallas_call(
        _ssd_chunk_kernel,
        grid=(t // CHUNK,),
        in_specs=[pl.BlockSpec((CHUNK, CONV_DIM), lambda c: (c, 0)),
                  pl.BlockSpec((CHUNK, SMALL_W), lambda c: (c, P_SMALL // SMALL_W)),
                  pl.BlockSpec((CHUNK, D_INNER), lambda c: (c, P_Z // D_INNER)),
                  pl.BlockSpec((1, LANES), lambda c: (0, 0)),
                  pl.BlockSpec((1, LANES), lambda c: (0, 0)),
                  pl.BlockSpec((1, D_INNER), lambda c: (0, 0)),
                  pl.BlockSpec((1, D_INNER), lambda c: (0, 0))],
        out_specs=[pl.BlockSpec((CHUNK, D_INNER), lambda c: (c, 0)),
                   pl.BlockSpec((PAIRS, LANES, SSM_N), lambda c: (0, 0, 0))],
        out_shape=[jax.ShapeDtypeStruct((t, D_INNER), BF16),
                   jax.ShapeDtypeStruct((PAIRS, LANES, SSM_N), F32)],
        scratch_shapes=[pltpu.VMEM((PAIRS, SSM_N, LANES), F32)],
        compiler_params=_params(1, 32),
        name="ssd_chunked",
    )(xbc_act, proj, proj, dt_bias_pad, a_pad, d_skip_cols, norm_w)


def _dt_prep_kernel(sm_ref, dtb_ref, a_ref, dt_ref, decay_ref):
    dt = _softplus(sm_ref[:, LANES:2 * LANES] + dtb_ref[...])
    dt_ref[...] = dt
    decay_ref[...] = jnp.exp(dt * a_ref[...])


def _dt_prep(proj, dt_bias_pad, a_pad):
    n = proj.shape[0]
    return pl.pallas_call(
        _dt_prep_kernel,
        grid=(1,),
        in_specs=[pl.BlockSpec((n, SMALL_W), lambda i: (0, P_SMALL // SMALL_W)),
                  pl.BlockSpec((1, LANES), lambda i: (0, 0)),
                  pl.BlockSpec((1, LANES), lambda i: (0, 0))],
        out_specs=[pl.BlockSpec((n, LANES), lambda i: (0, 0)), pl.BlockSpec((n, LANES), lambda i: (0, 0))],
        out_shape=[jax.ShapeDtypeStruct((n, LANES), F32), jax.ShapeDtypeStruct((n, LANES), F32)],
        compiler_params=_params(1, 32),
        name="dt_prep",
    )(proj, dt_bias_pad, a_pad)


def _ssd_step_kernel(decay_ref, st_ref, x_ref, b_ref, c_ref, dt_ref, dsk_ref, y_ref, so_ref):
    k = pl.program_id(0)
    n = x_ref.shape[0]
    head = lax.broadcasted_iota(jnp.int32, (LANES, LANES), 0)
    lane = lax.broadcasted_iota(jnp.int32, (LANES, LANES), 1)
    pick = (head == 2 * k + (lane >= SSM_P).astype(jnp.int32)).astype(F32)
    dt2 = _dot_exact(dt_ref[...], pick)
    x2 = x_ref[...]
    xd_t = (x2 * dt2).T
    c_t = c_ref[...].T.astype(BF16)
    ycols = jnp.zeros((LANES, n), F32)
    lane_n = lax.broadcasted_iota(jnp.int32, (LANES, n), 1)
    for t in range(n):
        upd = xd_t[:, t:t + 1] * b_ref[t:t + 1, :]
        new = []
        for hh in range(2):
            decay = decay_ref[t * SSM_HEADS + 2 * k + hh]
            new.append(st_ref[t, hh] * decay + upd[hh * SSM_P:(hh + 1) * SSM_P, :])
            so_ref[t, hh] = new[hh]
        y_all = _dot(jnp.concatenate(new, axis=0).astype(BF16), c_t)
        ycols = jnp.where(lane_n == t, y_all, ycols)
    y_ref[...] = ycols.T + dsk_ref[...] * x2


def _ssd_step(state, xbc_act, dt, decay, d_skip_cols):
    n = xbc_act.shape[0]
    assert n == LANES, "the per-sample SSD step transposes one (128, 128) tile of tokens"
    b0 = D_INNER // SSM_N
    c0 = (D_INNER + GN) // SSM_N
    st_spec = pl.BlockSpec((n, 2, SSM_P, SSM_N), lambda k: (0, k, 0, 0))
    return pl.pallas_call(
        _ssd_step_kernel,
        grid=(PAIRS,),
        in_specs=[pl.BlockSpec(memory_space=pltpu.SMEM),
                  st_spec,
                  pl.BlockSpec((n, LANES), lambda k: (0, k)),
                  pl.BlockSpec((n, SSM_N), lambda k: (0, b0 + k // PAIRS_PER_GROUP)),
                  pl.BlockSpec((n, SSM_N), lambda k: (0, c0 + k // PAIRS_PER_GROUP)),
                  pl.BlockSpec((n, LANES), lambda k: (0, 0)),
                  pl.BlockSpec((1, LANES), lambda k: (0, k))],
        out_specs=[pl.BlockSpec((n, LANES), lambda k: (0, k)), st_spec],
        out_shape=[jax.ShapeDtypeStruct((n, D_INNER), F32),
                   jax.ShapeDtypeStruct(state.shape, F32)],
        compiler_params=_params(1, 48),
        name="ssd_step",
    )(decay[:, :SSM_HEADS].reshape(-1), state, xbc_act, xbc_act, xbc_act, dt, d_skip_cols)


def _gate_norm_kernel(y_ref, z_ref, w_ref, o_ref):
    gw = D_INNER // SSM_GROUPS
    for g in range(SSM_GROUPS):
        sl = slice(g * gw, (g + 1) * gw)
        v = y_ref[:, sl] * _silu(z_ref[:, sl])
        o_ref[:, sl] = _rms(v, w_ref[:, sl]).astype(BF16)


def _gate_norm(y, proj, w):
    t = y.shape[0]
    tm = min(t, 512)
    return pl.pallas_call(
        _gate_norm_kernel,
        grid=(t // tm,),
        in_specs=[pl.BlockSpec((tm, D_INNER), lambda i: (i, 0)),
                  pl.BlockSpec((tm, D_INNER), lambda i: (i, P_Z // D_INNER)),
                  pl.BlockSpec((1, D_INNER), lambda i: (0, 0))],
        out_specs=pl.BlockSpec((tm, D_INNER), lambda i: (i, 0)),
        out_shape=jax.ShapeDtypeStruct((t, D_INNER), BF16),
        compiler_params=_params(1, 48),
        name="gate_norm",
    )(y, proj, w)


def _merge_kernel(oa_ref, os_ref, wa_ref, ws_ref, ga_ref, gs_ref, o_ref):
    a = _dot(oa_ref[...], wa_ref[...])
    s = _dot(os_ref[...], ws_ref[...])
    o_ref[...] = (jax.nn.sigmoid(ga_ref[...]) * a + jax.nn.sigmoid(gs_ref[...]) * s).astype(BF16)


def _merge(o_attn, o_ssm, w_attn, w_ssm, proj):
    t = o_attn.shape[0]
    tm, tn = min(t, 1024), 512
    return pl.pallas_call(
        _merge_kernel,
        grid=(t // tm, D_MODEL // tn),
        in_specs=[pl.BlockSpec((tm, HEADS * V_DIM), lambda i, j: (i, 0)),
                  pl.BlockSpec((tm, D_INNER), lambda i, j: (i, 0)),
                  pl.BlockSpec((HEADS * V_DIM, tn), lambda i, j: (0, j)),
                  pl.BlockSpec((D_INNER, tn), lambda i, j: (0, j)),
                  pl.BlockSpec((tm, tn), lambda i, j: (i, P_GA // tn + j)),
                  pl.BlockSpec((tm, tn), lambda i, j: (i, P_GS // tn + j))],
        out_specs=pl.BlockSpec((tm, tn), lambda i, j: (i, j)),
        out_shape=jax.ShapeDtypeStruct((t, D_MODEL), BF16),
        compiler_params=_params(2, 48),
        name="merge",
    )(o_attn, o_ssm, w_attn, w_ssm, proj, proj)


def _out_proj_kernel(m_ref, w_ref, h_ref, npost_ref, npre_ref, h1_ref, hn_ref):
    h1 = h_ref[...] + _rms(_dot(m_ref[...], w_ref[...]), npost_ref[...])
    h1_ref[...] = h1
    hn_ref[...] = _rms(h1, npre_ref[...]).astype(BF16)


def _out_proj(merged, w_out, h, n_post, n_pre):
    t = h.shape[0]
    tm = min(t, 512)
    row = lambda i: (i, 0)
    fixed = lambda i: (0, 0)
    return pl.pallas_call(
        _out_proj_kernel,
        grid=(t // tm,),
        in_specs=[pl.BlockSpec((tm, D_MODEL), row), pl.BlockSpec((D_MODEL, D_MODEL), fixed),
                  pl.BlockSpec((tm, D_MODEL), row), pl.BlockSpec((1, D_MODEL), fixed),
                  pl.BlockSpec((1, D_MODEL), fixed)],
        out_specs=[pl.BlockSpec((tm, D_MODEL), row), pl.BlockSpec((tm, D_MODEL), row)],
        out_shape=[jax.ShapeDtypeStruct((t, D_MODEL), F32), jax.ShapeDtypeStruct((t, D_MODEL), BF16)],
        compiler_params=_params(1, 56),
        name="out_proj",
    )(merged, w_out, h, n_post, n_pre)


FFN_HALO = BF16_SUBLANES
FFN_ROW_CHUNKS = 2


def _ffn_kernel(x_ref, halo_ref, wg_ref, wv_ref, cwg_ref, cwv_ref, cbg_ref, cbv_ref, wd_ref, h_ref, nw_ref,
                o_ref, tg_ref, tv_ref, xe_s, u_s, acc_s):
    i, j = pl.program_id(0), pl.program_id(1)
    tm, tf = x_ref.shape[0], wg_ref.shape[1]

    @pl.when(j == 0)
    def _():
        halo = halo_ref[...]
        xe_s[0:FFN_HALO, :] = jnp.where(i == 0, jnp.zeros_like(halo), halo)
        xe_s[FFN_HALO:, :] = x_ref[...]
        acc_s[...] = jnp.zeros(acc_s.shape, F32)

    base = FFN_HALO - (FFN_CONV - 1)
    rc = tm // FFN_ROW_CHUNKS
    for c in range(FFN_ROW_CHUNKS):
        r0 = 0 if c == 0 else FFN_HALO + c * rc
        r1 = FFN_HALO + (c + 1) * rc
        xe = xe_s[r0:r1, :]
        u_s[r0:r1, 0:tf] = _dot(xe, wg_ref[...])
        u_s[r0:r1, tf:] = _dot(xe, wv_ref[...])
        t0 = base + c * rc

        def conv(lo, w_ref, b_ref):
            w = w_ref[...]
            y = u_s[t0:t0 + rc, lo:lo + tf] * w[0:1]
            for k in range(1, FFN_CONV):
                y = y + u_s[t0 + k:t0 + k + rc, lo:lo + tf] * w[k:k + 1]
            return y + b_ref[...]

        f = (_gelu_tanh(conv(0, cwg_ref, cbg_ref)) * conv(tf, cwv_ref, cbv_ref)).astype(BF16)
        acc_s[c * rc:(c + 1) * rc, :] += _dot(f, wd_ref[...])

    tail = FFN_HALO + tm - SUBLANES
    tg_ref[...] = u_s[tail:tail + SUBLANES, 0:tf]
    tv_ref[...] = u_s[tail:tail + SUBLANES, tf:]

    @pl.when(j == pl.num_programs(1) - 1)
    def _():
        o_ref[...] = h_ref[...] + _rms(acc_s[...], nw_ref[...])


def _ffn(hn, h, w_up, conv_w, conv_b, w_down, nw):
    t = h.shape[0]
    tm, tf = min(t, 512), 1024
    once = pl.Buffered(1)
    nj = D_FF // tf
    gate = lambda i, j: (0, j)
    value = lambda i, j: (0, nj + j)
    row = lambda i, j: (i, 0)
    return pl.pallas_call(
        _ffn_kernel,
        grid=(t // tm, nj),
        in_specs=[pl.BlockSpec((tm, D_MODEL), row),
                  pl.BlockSpec((FFN_HALO, D_MODEL), lambda i, j: (jnp.maximum(i * (tm // FFN_HALO) - 1, 0), 0)),
                  pl.BlockSpec((D_MODEL, tf), gate), pl.BlockSpec((D_MODEL, tf), value),
                  pl.BlockSpec((FFN_CONV, tf), gate), pl.BlockSpec((FFN_CONV, tf), value),
                  pl.BlockSpec((1, tf), gate), pl.BlockSpec((1, tf), value),
                  pl.BlockSpec((tf, D_MODEL), lambda i, j: (j, 0)),
                  pl.BlockSpec((tm, D_MODEL), row, pipeline_mode=once),
                  pl.BlockSpec((1, D_MODEL), lambda i, j: (0, 0))],
        out_specs=[pl.BlockSpec((tm, D_MODEL), row, pipeline_mode=once),
                   pl.BlockSpec((SUBLANES, tf), lambda i, j: (i, j)),
                   pl.BlockSpec((SUBLANES, tf), lambda i, j: (i, j))],
        out_shape=[jax.ShapeDtypeStruct((t, D_MODEL), F32),
                   jax.ShapeDtypeStruct((t // tm * SUBLANES, D_FF), F32),
                   jax.ShapeDtypeStruct((t // tm * SUBLANES, D_FF), F32)],
        scratch_shapes=[pltpu.VMEM((tm + FFN_HALO, D_MODEL), BF16),
                        pltpu.VMEM((tm + FFN_HALO, 2 * tf), F32),
                        pltpu.VMEM((tm, D_MODEL), F32)],
        compiler_params=_params(2, 58),
        name="ffn",
    )(hn, hn, w_up, w_up, conv_w, conv_w, conv_b, conv_b, w_down, h, nw)


def _ffn_step_kernel(x_ref, wg_ref, wv_ref, sg_ref, sv_ref, cwg_ref, cwv_ref, cbg_ref, cbv_ref, wd_ref, h_ref,
                     nw_ref, o_ref, ug_ref, uv_ref, acc_s):
    j = pl.program_id(0)
    x = x_ref[...]

    def conv(u, s_ref, w_ref, b_ref):
        w = w_ref[...]
        y = s_ref[0] * w[0:1]
        for k in range(1, FFN_CONV - 1):
            y = y + s_ref[k] * w[k:k + 1]
        return y + u * w[FFN_CONV - 1:FFN_CONV] + b_ref[...]

    ug = _dot(x, wg_ref[...])
    uv = _dot(x, wv_ref[...])
    ug_ref[...] = ug
    uv_ref[...] = uv
    f = (_gelu_tanh(conv(ug, sg_ref, cwg_ref, cbg_ref)) * conv(uv, sv_ref, cwv_ref, cbv_ref)).astype(BF16)
    d = _dot(f, wd_ref[...])

    @pl.when(j == 0)
    def _():
        acc_s[...] = d

    @pl.when(j > 0)
    def _():
        acc_s[...] += d

    @pl.when(j == pl.num_programs(0) - 1)
    def _():
        o_ref[...] = h_ref[...] + _rms(acc_s[...], nw_ref[...])


def _ffn_step(hn, h, state_t, w_up, conv_w, conv_b, w_down, nw):
    n = h.shape[0]
    tf = 512
    nj = D_FF // tf
    gate = lambda j: (0, j)
    value = lambda j: (0, nj + j)
    fixed = lambda j: (0, 0)
    return pl.pallas_call(
        _ffn_step_kernel,
        grid=(nj,),
        in_specs=[pl.BlockSpec((n, D_MODEL), fixed),
                  pl.BlockSpec((D_MODEL, tf), gate), pl.BlockSpec((D_MODEL, tf), value),
                  pl.BlockSpec((FFN_CONV - 1, n, tf), lambda j: (0, 0, j)),
                  pl.BlockSpec((FFN_CONV - 1, n, tf), lambda j: (0, 0, nj + j)),
                  pl.BlockSpec((FFN_CONV, tf), gate), pl.BlockSpec((FFN_CONV, tf), value),
                  pl.BlockSpec((1, tf), gate), pl.BlockSpec((1, tf), value),
                  pl.BlockSpec((tf, D_MODEL), lambda j: (j, 0)),
                  pl.BlockSpec((n, D_MODEL), fixed),
                  pl.BlockSpec((1, D_MODEL), fixed)],
        out_specs=[pl.BlockSpec((n, D_MODEL), fixed), pl.BlockSpec((n, tf), gate), pl.BlockSpec((n, tf), gate)],
        out_shape=[jax.ShapeDtypeStruct((n, D_MODEL), F32),
                   jax.ShapeDtypeStruct((n, D_FF), F32), jax.ShapeDtypeStruct((n, D_FF), F32)],
        scratch_shapes=[pltpu.VMEM((n, D_MODEL), F32)],
        compiler_params=_params(1, 32),
        name="ffn_step",
    )(hn, w_up, w_up, state_t, state_t, conv_w, conv_w, conv_b, conv_b, w_down, h, nw)


def _ple_kernel(h_ref, pe_ref, wg_ref, wp_ref, nw_ref, o_ref):
    h = h_ref[...]
    gate = jax.nn.sigmoid(_dot(h.astype(BF16), wg_ref[...]))
    o_ref[...] = h + _rms(gate * _dot(pe_ref[...].astype(BF16), wp_ref[...]), nw_ref[...])


def _ple(h, pe, w_gate, w_proj, nw):
    t = h.shape[0]
    tm = min(t, 512)
    row = lambda i: (i, 0)
    fixed = lambda i: (0, 0)
    return pl.pallas_call(
        _ple_kernel,
        grid=(t // tm,),
        in_specs=[pl.BlockSpec((tm, D_MODEL), row), pl.BlockSpec((tm, PLE_DIM), row),
                  pl.BlockSpec((D_MODEL, D_MODEL), fixed), pl.BlockSpec((PLE_DIM, D_MODEL), fixed),
                  pl.BlockSpec((1, D_MODEL), fixed)],
        out_specs=pl.BlockSpec((tm, D_MODEL), row),
        out_shape=jax.ShapeDtypeStruct((t, D_MODEL), F32),
        compiler_params=_params(1, 56),
        name="ple",
    )(h, pe, w_gate, w_proj, nw)


def _rot_half_cols(w):
    half = QK_ROPE // 2
    return jnp.concatenate([-w[..., half:], w[..., :half]], axis=-1)


def _prepare_weights(w):
    w_in_t = w['w_in'].T
    splits = np.cumsum([0, Q_LORA, KV_LORA, QK_ROPE, D_INNER, CONV_DIM, SSM_HEADS, D_MODEL, D_MODEL])
    q_c, kv_c, k_r, z, xbc, dt, g_a, g_s = [w_in_t[a:b] for a, b in zip(splits[:-1], splits[1:])]
    pad = jnp.zeros((SMALL_W - 2 * QK_ROPE - SSM_HEADS, D_MODEL), F32)
    k_r_rot = _rot_half_cols(k_r.T).T
    packed_t = jnp.concatenate([z, g_a, g_s, xbc, kv_c, k_r, k_r_rot, dt, pad, q_c], axis=0)
    wq = w['w_q_up'].reshape(Q_LORA, HEADS, QK_NOPE + QK_ROPE)
    wq_rope = wq[:, :, QK_NOPE:]
    wkv = w['w_kv_up'].reshape(KV_LORA, HEADS, QK_NOPE + V_DIM)
    lane_pad = LANES - SSM_HEADS
    return dict(
        w_in=packed_t.astype(BF16),
        wq_nope=wq[:, :, :QK_NOPE].reshape(Q_LORA, HEADS * QK_NOPE).astype(BF16),
        wq_rope=wq_rope.reshape(Q_LORA, HEADS * QK_ROPE).astype(BF16),
        wq_rope_rot=_rot_half_cols(wq_rope).reshape(Q_LORA, HEADS * QK_ROPE).astype(BF16),
        w_nope=wkv[:, :, :QK_NOPE].transpose(1, 2, 0).astype(BF16),
        w_v=wkv[:, :, QK_NOPE:].transpose(1, 0, 2).astype(BF16),
        w_branch_attn=w['w_branch_attn'].astype(BF16),
        w_branch_ssm=w['w_branch_ssm'].astype(BF16),
        w_out=w['w_out'].astype(BF16),
        w_ffn_up=w['w_ffn_up'].astype(BF16),
        w_ffn_down=w['w_ffn_down'].astype(BF16),
        w_ple_gate=w['w_ple_gate'].astype(BF16),
        w_ple_proj=w['w_ple_proj'].astype(BF16),
        dt_bias=jnp.pad(w['dt_bias'], (0, lane_pad))[None, :],
        a_neg=jnp.pad(-jnp.exp(w['a_log']), (0, lane_pad))[None, :],
        d_skip=jnp.repeat(w['d_skip'], SSM_P)[None, :],
    )


def _rope_tables(pos):
    half = QK_ROPE // 2
    inv = ROPE_THETA ** (-jnp.arange(half, dtype=F32) / half)
    ang = pos[:, None] * inv[None, :]
    cos, sin = jnp.cos(ang), jnp.sin(ang)
    cos64, sin64 = jnp.concatenate([cos, cos], axis=1), jnp.concatenate([sin, sin], axis=1)
    return cos64, sin64, jnp.concatenate([cos64, cos64], axis=1), jnp.concatenate([sin64, sin64], axis=1)


def _vec(v):
    return v[None, :]


def _mixer_front(h, pos, w, pw):
    proj = _in_proj(h, _vec(w['norm_pre_mix']), pw['w_in'])
    cos64, sin64, cos128, sin128 = _rope_tables(pos)
    c, c_bf, kr, kr_bf = _kv_prep(proj, cos64, sin64, _vec(w['norm_kv']))
    q_nope, q_rope = _q_prep(proj, _vec(w['norm_q']), pw['wq_nope'], pw['wq_rope'], pw['wq_rope_rot'],
                             cos128, sin128)
    return proj, c, c_bf, kr, kr_bf, q_nope, q_rope


def _mixer_back(h, proj, o_attn, o_ssm, w, pw):
    merged = _merge(o_attn, o_ssm, pw['w_branch_attn'], pw['w_branch_ssm'], proj)
    return _out_proj(merged, pw['w_out'], h, _vec(w['norm_post_mix']), _vec(w['norm_pre_ffn']))


def _prompt_layer(h, pe, w, pw):
    t = h.shape[0]
    pos = jnp.arange(t, dtype=F32)
    proj, c, c_bf, kr, kr_bf, q_nope, q_rope = _mixer_front(h, pos, w, pw)
    o_attn = _prompt_attn(q_nope, q_rope, pw['w_nope'], pw['w_v'], c_bf, kr_bf)
    xbc_act = _ssm_conv(proj, w['conv_ssm_w'], _vec(w['conv_ssm_b']))
    o_ssm, st = _ssd_chunked(xbc_act, proj, pw['dt_bias'], pw['a_neg'], pw['d_skip'], _vec(w['norm_ssm']))
    h1, hn1 = _mixer_back(h, proj, o_attn, o_ssm, w, pw)
    h2, tail_g, tail_v = _ffn(hn1, h1, pw['w_ffn_up'], w['conv_ffn_w'], _vec(w['conv_ffn_b']),
                              pw['w_ffn_down'], _vec(w['norm_post_ffn']))
    h3 = _ple(h2, pe, pw['w_ple_gate'], pw['w_ple_proj'], _vec(w['norm_ple']))
    ssm_conv_new = proj[t - (SSM_CONV - 1):, P_XBC:P_XBC + CONV_DIM]
    last = tail_g.shape[0] - (FFN_CONV - 1)
    ffn_conv_new = jnp.concatenate([tail_g[last:], tail_v[last:]], axis=1)
    return h3, (c, kr, st.reshape(SSM_HEADS, SSM_P, SSM_N), ssm_conv_new, ffn_conv_new)


def _sample_layer(h, pe, past, w, pw, cache_c, cache_kr, layer, page_table, ssm_state, ssm_conv_state,
                  ffn_conv_state):
    n = h.shape[0]
    pos = jnp.full((n,), past, F32)
    proj, c, c_bf, kr, kr_bf, q_nope, q_rope = _mixer_front(h, pos, w, pw)
    q_lat = _q_latent_sample(q_nope, pw['w_nope']).transpose(1, 0, 2)
    o_lat = _decode_attn(page_table, q_lat, q_rope.reshape(n, HEADS, QK_ROPE), c_bf[:, None, :],
                         kr_bf[:, None, :], cache_c, jnp.swapaxes(cache_kr, 2, 3), layer)
    o_attn = _o_attn_sample(o_lat.transpose(1, 0, 2), pw['w_v'])
    xbc_act = _ssm_conv_step(ssm_conv_state.transpose(1, 0, 2), proj, w['conv_ssm_w'], _vec(w['conv_ssm_b']))
    dt, decay = _dt_prep(proj, pw['dt_bias'], pw['a_neg'])
    y_ssm, st = _ssd_step(ssm_state, xbc_act, dt, decay, pw['d_skip'])
    o_ssm = _gate_norm(y_ssm, proj, _vec(w['norm_ssm']))
    h1, hn1 = _mixer_back(h, proj, o_attn, o_ssm, w, pw)
    h2, u_g, u_v = _ffn_step(hn1, h1, ffn_conv_state.transpose(1, 0, 2), pw['w_ffn_up'], w['conv_ffn_w'],
                             _vec(w['conv_ffn_b']), pw['w_ffn_down'], _vec(w['norm_post_ffn']))
    h3 = _ple(h2, pe, pw['w_ple_gate'], pw['w_ple_proj'], _vec(w['norm_ple']))
    xbc_raw = proj[:, P_XBC:P_XBC + CONV_DIM]
    ssm_conv_new = jnp.concatenate([ssm_conv_state[:, 1:], xbc_raw[:, None, :]], axis=1)
    u = jnp.concatenate([u_g, u_v], axis=1)
    ffn_conv_new = jnp.concatenate([ffn_conv_state[:, 1:], u[:, None, :]], axis=1)
    return h3, (c[:, None, :], kr[:, None, :], st, ssm_conv_new, ffn_conv_new)


def kernel(x_prompt, x_sample, cache_kv_latent, cache_k_rope, state_ssm, state_ssm_conv, state_ffn_conv, page_table, p_prompt, p_sample, norm_pre_mix, w_in, norm_q, w_q_up, norm_kv, w_kv_up, w_branch_attn, conv_ssm_w, conv_ssm_b, dt_bias, a_log, d_skip, norm_ssm, w_branch_ssm, w_out, norm_post_mix, norm_pre_ffn, w_ffn_up, conv_ffn_w, conv_ffn_b, w_ffn_down, norm_post_ffn, w_ple_gate, w_ple_proj, norm_ple):
    assert x_prompt.shape[0] == 1, "one prompt sequence"
    assert x_sample.shape[1] == 1, "one new token per sample sequence"
    depth = w_in.shape[0]
    past = page_table.shape[1] * PAGE
    weights = dict(norm_pre_mix=norm_pre_mix, w_in=w_in, norm_q=norm_q, w_q_up=w_q_up, norm_kv=norm_kv,
                   w_kv_up=w_kv_up, w_branch_attn=w_branch_attn, conv_ssm_w=conv_ssm_w, conv_ssm_b=conv_ssm_b,
                   dt_bias=dt_bias, a_log=a_log, d_skip=d_skip, norm_ssm=norm_ssm, w_branch_ssm=w_branch_ssm,
                   w_out=w_out, norm_post_mix=norm_post_mix, norm_pre_ffn=norm_pre_ffn, w_ffn_up=w_ffn_up,
                   conv_ffn_w=conv_ffn_w, conv_ffn_b=conv_ffn_b, w_ffn_down=w_ffn_down,
                   norm_post_ffn=norm_post_ffn, w_ple_gate=w_ple_gate, w_ple_proj=w_ple_proj, norm_ple=norm_ple)
    hp, hs = x_prompt[0], x_sample[:, 0]
    outs_p, outs_s = [], []
    for d in range(depth):
        w = {k: v[d] for k, v in weights.items()}
        pw = _prepare_weights(w)
        hp, st_p = _prompt_layer(hp, p_prompt[d, 0], w, pw)
        hs, st_s = _sample_layer(hs, p_sample[d, :, 0], past, w, pw, cache_kv_latent, cache_k_rope, d,
                                 page_table, state_ssm[d], state_ssm_conv[d], state_ffn_conv[d])
        outs_p.append(st_p)
        outs_s.append(st_s)
    stack_p = [jnp.stack([o[k] for o in outs_p])[:, None] for k in range(5)]
    stack_s = [jnp.stack([o[k] for o in outs_s]) for k in range(5)]
    return (hp[None], hs[:, None], *stack_p, *stack_s)
```
